```python
import jax, jax.numpy as jnp
from jax import lax
import numpy as np

D_MODEL = 2048
BATCH = 4
SEQ = 8192
DEPTH = 1

PLE_DIM = 256
MIX_WIDTH = D_MODEL
LRU_WIDTH = MIX_WIDTH // 2
LRU_BLOCKS = 16
LRU_BLOCK_DIM = LRU_WIDTH // LRU_BLOCKS
CONV_WIDTH = 4
LRU_C = 8.0
MLA_HEADS = 8
QK_NOPE_DIM = 128
QK_ROPE_DIM = 64
V_HEAD_DIM = 128
Q_LORA_RANK = 512
KV_LORA_RANK = 512
ROPE_THETA = 10000.0
Q_BLOCK = 128
D_FF = 5632
LN_EPS = 1e-5
RMS_EPS = 1e-6
DEEPNORM_ALPHA = (2 * DEPTH) ** 0.25
DEEPNORM_BETA = (8 * DEPTH) ** -0.25
IN_SPLITS = (LRU_WIDTH, 2 * LRU_WIDTH, 2 * LRU_WIDTH + Q_LORA_RANK,
             2 * LRU_WIDTH + Q_LORA_RANK + KV_LORA_RANK)
IN_PROJ_DIM = 2 * LRU_WIDTH + Q_LORA_RANK + KV_LORA_RANK + QK_ROPE_DIM

kernel_name = "hymba_rglru_mla_macaron_deepnorm"


def layer_norm(x, g, b):
    xf = x.astype(jnp.float32)
    mu = jnp.mean(xf, axis=-1, keepdims=True)
    xc = xf - mu
    var = jnp.mean(xc * xc, axis=-1, keepdims=True)
    y = xc * lax.rsqrt(var + LN_EPS) * g.astype(jnp.float32) + b.astype(jnp.float32)
    return y.astype(x.dtype)


def rms_norm(x, g):
    xf = x.astype(jnp.float32)
    y = xf * lax.rsqrt(jnp.mean(xf * xf, axis=-1, keepdims=True) + RMS_EPS) * g.astype(jnp.float32)
    return y.astype(x.dtype)


def swiglu(x, w_gate, w_up, w_down):
    return (jax.nn.silu(x @ w_gate) * (x @ w_up)) @ w_down


def rope_tables(positions):
    inv_freq = ROPE_THETA ** (-jnp.arange(0, QK_ROPE_DIM, 2, dtype=jnp.float32) / QK_ROPE_DIM)
    ang = positions.astype(jnp.float32)[..., None] * inv_freq
    return jnp.cos(ang), jnp.sin(ang)


def apply_rope(x, cos, sin):
    xf = x.astype(jnp.float32)
    x1, x2 = jnp.split(xf, 2, axis=-1)
    y = jnp.concatenate([x1 * cos - x2 * sin, x2 * cos + x1 * sin], axis=-1)
    return y.astype(x.dtype)


def causal_dwconv(x, w, b):
    s = x.shape[1]
    xp = jnp.pad(x, ((0, 0), (CONV_WIDTH - 1, 0), (0, 0)))
    return b + sum(xp[:, k:k + s] * w[k] for k in range(CONV_WIDTH))


def rg_lru(x, w_a, b_a, w_x, b_x, lam):
    bsz, s, _ = x.shape
    xb = x.reshape(bsz, s, LRU_BLOCKS, LRU_BLOCK_DIM)
    rec_gate = jax.nn.sigmoid(jnp.einsum('bsgi,gij->bsgj', xb, w_a) + b_a).reshape(bsz, s, LRU_WIDTH)
    in_gate = jax.nn.sigmoid(jnp.einsum('bsgi,gij->bsgj', xb, w_x) + b_x).reshape(bsz, s, LRU_WIDTH)
    log_a = -LRU_C * rec_gate.astype(jnp.float32) * jax.nn.softplus(-lam.astype(jnp.float32))
    a = jnp.exp(log_a)
    u = jnp.sqrt(-jnp.expm1(2.0 * log_a)) * (in_gate * x).astype(jnp.float32)

    def combine(left, right):
        a_l, h_l = left
        a_r, h_r = right
        return a_l * a_r, a_r * h_l + h_r

    _, h = lax.associative_scan(combine, (a, u), axis=1)
    return h.astype(x.dtype)


def mla_attention(c_q, c_kv, k_pe_raw, cos, sin, q_norm_g, w_q_up, kv_norm_g, w_kv_up):
    bsz, s, _ = c_q.shape
    q = (rms_norm(c_q, q_norm_g) @ w_q_up).reshape(bsz, s, MLA_HEADS, QK_NOPE_DIM + QK_ROPE_DIM)
    q_nope, q_pe = q[..., :QK_NOPE_DIM], q[..., QK_NOPE_DIM:]
    q_pe = apply_rope(q_pe, cos[:, :, None, :], sin[:, :, None, :])
    k_pe = apply_rope(k_pe_raw, cos, sin)
    kv = (rms_norm(c_kv, kv_norm_g) @ w_kv_up).reshape(bsz, s, MLA_HEADS, QK_NOPE_DIM + V_HEAD_DIM)
    k_nope, v = kv[..., :QK_NOPE_DIM], kv[..., QK_NOPE_DIM:]
    scale = (QK_NOPE_DIM + QK_ROPE_DIM) ** -0.5
    outs = []
    for blk in range(s // Q_BLOCK):
        q0 = blk * Q_BLOCK
        q1 = q0 + Q_BLOCK
        sc = (jnp.einsum('bqhd,bkhd->bhqk', q_nope[:, q0:q1], k_nope[:, :q1])
              + jnp.einsum('bqhr,bkr->bhqk', q_pe[:, q0:q1], k_pe[:, :q1]))
        sc = sc.astype(jnp.float32) * scale
        mask = jnp.arange(q1)[None, :] <= (q0 + jnp.arange(Q_BLOCK))[:, None]
        prob = jax.nn.softmax(jnp.where(mask, sc, -jnp.inf), axis=-1).astype(v.dtype)
        outs.append(jnp.einsum('bhqk,bkhd->bqhd', prob, v[:, :q1]))
    o = jnp.concatenate(outs, axis=1)
    return o.reshape(bsz, s, MLA_HEADS * V_HEAD_DIM)


def setup_inputs(seed: int = 0) -> dict:
    key = jax.random.key(seed)
    ks = iter(jax.random.split(key, 40))

    def nrm(shape, scale):
        return jax.random.normal(next(ks), shape, jnp.float32) * scale

    def gain(shape):
        return 1.0 + nrm(shape, 0.01)

    L = DEPTH
    u = jax.random.uniform(next(ks), (L, LRU_WIDTH), jnp.float32, minval=0.9, maxval=0.999)
    a0 = u ** (1.0 / LRU_C)
    lru_lambda = jnp.log(a0) - jnp.log1p(-a0)
    return {
        "x": nrm((BATCH, SEQ, D_MODEL), 1.0),
        "p": nrm((DEPTH, BATCH, SEQ, PLE_DIM), 1.0),
        "positions": jnp.broadcast_to(jnp.arange(SEQ, dtype=jnp.int32), (BATCH, SEQ)),
        "ffn1_w_gate": nrm((L, D_MODEL, D_FF), D_MODEL ** -0.5),
        "ffn1_w_up": nrm((L, D_MODEL, D_FF), D_MODEL ** -0.5),
        "ffn1_w_down": nrm((L, D_FF, D_MODEL), DEEPNORM_BETA * D_FF ** -0.5),
        "ln1_g": gain((L, D_MODEL)),
        "ln1_b": nrm((L, D_MODEL), 0.01),
        "w_in": nrm((L, D_MODEL, IN_PROJ_DIM), D_MODEL ** -0.5),
        "conv_w": nrm((L, CONV_WIDTH, LRU_WIDTH), CONV_WIDTH ** -0.5),
        "conv_b": nrm((L, LRU_WIDTH), 0.01),
        "lru_w_a": nrm((L, LRU_BLOCKS, LRU_BLOCK_DIM, LRU_BLOCK_DIM), LRU_BLOCK_DIM ** -0.5),
        "lru_b_a": nrm((L, LRU_BLOCKS, LRU_BLOCK_DIM), 0.01),
        "lru_w_x": nrm((L, LRU_BLOCKS, LRU_BLOCK_DIM, LRU_BLOCK_DIM), LRU_BLOCK_DIM ** -0.5),
        "lru_b_x": nrm((L, LRU_BLOCKS, LRU_BLOCK_DIM), 0.01),
        "lru_lambda": lru_lambda,
        "q_norm_g": gain((L, Q_LORA_RANK)),
        "w_q_up": nrm((L, Q_LORA_RANK, MLA_HEADS * (QK_NOPE_DIM + QK_ROPE_DIM)), Q_LORA_RANK ** -0.5),
        "kv_norm_g": gain((L, KV_LORA_RANK)),
        "w_kv_up": nrm((L, KV_LORA_RANK, MLA_HEADS * (QK_NOPE_DIM + V_HEAD_DIM)), KV_LORA_RANK ** -0.5),
        "w_out": nrm((L, MIX_WIDTH, D_MODEL), DEEPNORM_BETA * MIX_WIDTH ** -0.5),
        "ln2_g": gain((L, D_MODEL)),
        "ln2_b": nrm((L, D_MODEL), 0.01),
        "ffn2_w_gate": nrm((L, D_MODEL, D_FF), D_MODEL ** -0.5),
        "ffn2_w_up": nrm((L, D_MODEL, D_FF), D_MODEL ** -0.5),
        "ffn2_w_down": nrm((L, D_FF, D_MODEL), DEEPNORM_BETA * D_FF ** -0.5),
        "ln3_g": gain((L, D_MODEL)),
        "ln3_b": nrm((L, D_MODEL), 0.01),
        "ple_w_gate": nrm((L, D_MODEL, D_MODEL), D_MODEL ** -0.5),
        "ple_b_gate": nrm((L, D_MODEL), 0.01),
        "ple_w_proj": nrm((L, PLE_DIM, D_MODEL), DEEPNORM_BETA * PLE_DIM ** -0.5),
        "ln4_g": gain((L, D_MODEL)),
        "ln4_b": nrm((L, D_MODEL), 0.01),
    }


def reference(x, p, positions, ffn1_w_gate, ffn1_w_up, ffn1_w_down, ln1_g, ln1_b, w_in,
              conv_w, conv_b, lru_w_a, lru_b_a, lru_w_x, lru_b_x, lru_lambda, q_norm_g, w_q_up,
              kv_norm_g, w_kv_up, w_out, ln2_g, ln2_b, ffn2_w_gate, ffn2_w_up, ffn2_w_down,
              ln3_g, ln3_b, ple_w_gate, ple_b_gate, ple_w_proj, ln4_g, ln4_b):
    cos, sin = rope_tables(positions)
    for i in range(DEPTH):
        x = layer_norm(DEEPNORM_ALPHA * x + 0.5 * swiglu(x, ffn1_w_gate[i], ffn1_w_up[i], ffn1_w_down[i]),
                       ln1_g[i], ln1_b[i])
        z = x @ w_in[i]
        lru_in, lru_gate, c_q, c_kv, k_pe_raw = jnp.split(z, IN_SPLITS, axis=-1)
        h = rg_lru(causal_dwconv(lru_in, conv_w[i], conv_b[i]),
                   lru_w_a[i], lru_b_a[i], lru_w_x[i], lru_b_x[i], lru_lambda[i])
        y_lru = jax.nn.gelu(lru_gate) * h
        y_mla = mla_attention(c_q, c_kv, k_pe_raw, cos, sin,
                              q_norm_g[i], w_q_up[i], kv_norm_g[i], w_kv_up[i])
        mix = jnp.concatenate([y_lru, y_mla], axis=-1) @ w_out[i]
        x = layer_norm(DEEPNORM_ALPHA * x + mix, ln2_g[i], ln2_b[i])
        x = layer_norm(DEEPNORM_ALPHA * x + 0.5 * swiglu(x, ffn2_w_gate[i], ffn2_w_up[i], ffn2_w_down[i]),
                       ln3_g[i], ln3_b[i])
        ple = jax.nn.sigmoid(x @ ple_w_gate[i] + ple_b_gate[i]) * (p[i].astype(x.dtype) @ ple_w_proj[i])
        x = layer_norm(DEEPNORM_ALPHA * x + ple, ln4_g[i], ln4_b[i])
    return x
```

```python
import functools
import math

import jax
import jax.numpy as jnp
import numpy as np
from jax import lax
from jax.experimental import pallas as pl
from jax.experimental.pallas import tpu as pltpu

LRU_BLOCKS = 16
CONV_WIDTH = 4
LRU_C = 8.0
MLA_HEADS = 8
QK_NOPE_DIM = 128
QK_ROPE_DIM = 64
V_HEAD_DIM = 128
ROPE_THETA = 10000.0
LN_EPS = 1e-5
RMS_EPS = 1e-6
DEPTH = 1
DEEPNORM_ALPHA = (2 * DEPTH) ** 0.25

LANES = 128
MXU_DIM = 256
VMEM_LIMIT = 56 * 1024 * 1024
LOG2E = math.log2(math.e)

BF16 = jnp.bfloat16
F32 = jnp.float32


def _params(*sem):
    return pltpu.CompilerParams(dimension_semantics=sem, vmem_limit_bytes=VMEM_LIMIT)


def _layer_norm(y, g, b):
    mu = jnp.mean(y, axis=-1, keepdims=True)
    yc = y - mu
    var = jnp.mean(yc * yc, axis=-1, keepdims=True)
    return yc * lax.rsqrt(var + LN_EPS) * g + b


def _rms_norm(y, g):
    return y * lax.rsqrt(jnp.mean(y * y, axis=-1, keepdims=True) + RMS_EPS) * g


def _dot(a, b):
    return jnp.dot(a, b, preferred_element_type=F32)


def _rope_kernel(inv_ref, pos_ref, cos_ref, sin_ref):
    j = pl.program_id(0)
    ang = pos_ref[...].astype(F32) * inv_ref[j]
    cos_ref[0] = jnp.cos(ang)
    sin_ref[0] = jnp.sin(ang)


def rope_tables(positions):
    n = positions.size
    half = QK_ROPE_DIM // 2
    inv_freq = ROPE_THETA ** (-jnp.arange(0, QK_ROPE_DIM, 2, dtype=F32) / QK_ROPE_DIM)
    pos2 = positions.reshape(n // LANES, LANES)
    shp = jax.ShapeDtypeStruct((half, n // LANES, LANES), F32)
    cos, sin = pl.pallas_call(
        _rope_kernel,
        grid=(half,),
        in_specs=[pl.BlockSpec(memory_space=pltpu.SMEM),
                  pl.BlockSpec((n // LANES, LANES), lambda j: (0, 0))],
        out_specs=[pl.BlockSpec((1, n // LANES, LANES), lambda j: (j, 0, 0))] * 2,
        out_shape=[shp, shp],
        compiler_params=_params("arbitrary"),
        name="rope_tables",
    )(inv_freq, pos2)
    cos = cos.reshape(half, n).T
    sin = sin.reshape(half, n).T
    return jnp.concatenate([cos, cos], -1), jnp.concatenate([-sin, sin], -1)


def _ffn_ln_kernel(x_ref, wg_ref, wu_ref, wd_ref, g_ref, b_ref, o_ref, xb_ref):
    j = pl.program_id(1)

    @pl.when(j == 0)
    def _():
        xb_ref[...] = x_ref[...].astype(BF16)

    xb = xb_ref[...]
    gate = _dot(xb, wg_ref[...])
    up = _dot(xb, wu_ref[...])
    h = (gate * jax.nn.sigmoid(gate) * up).astype(BF16)
    part = _dot(h, wd_ref[...])

    @pl.when(j == 0)
    def _():
        o_ref[...] = part

    @pl.when(j > 0)
    def _():
        o_ref[...] += part

    @pl.when(j == pl.num_programs(1) - 1)
    def _():
        y = DEEPNORM_ALPHA * x_ref[...] + 0.5 * o_ref[...]
        o_ref[...] = _layer_norm(y, g_ref[...], b_ref[...])


def ffn_ln(x, wg, wu, wd, g, b, *, tm, tf):
    n, d = x.shape
    f = wg.shape[1]
    return pl.pallas_call(
        _ffn_ln_kernel,
        grid=(n // tm, f // tf),
        in_specs=[pl.BlockSpec((tm, d), lambda i, j: (i, 0)),
                  pl.BlockSpec((d, tf), lambda i, j: (0, j)),
                  pl.BlockSpec((d, tf), lambda i, j: (0, j)),
                  pl.BlockSpec((tf, d), lambda i, j: (j, 0)),
                  pl.BlockSpec((1, d), lambda i, j: (0, 0)),
                  pl.BlockSpec((1, d), lambda i, j: (0, 0))],
        out_specs=pl.BlockSpec((tm, d), lambda i, j: (i, 0)),
        out_shape=jax.ShapeDtypeStruct((n, d), F32),
        scratch_shapes=[pltpu.VMEM((tm, d), BF16)],
        compiler_params=_params("parallel", "arbitrary"),
        name="ffn_ln",
    )(x, wg, wu, wd, g, b)


def _in_proj_kernel(x_ref, win_ref, wq_ref, wkv_ref, qg_ref, kvg_ref, cos_ref, sin_ref,
                    lru_ref, gate_ref, q_ref, k_ref, v_ref, *, lru_w, q_rank, kv_rank):
    xb = x_ref[...].astype(BF16)
    cos = cos_ref[...]
    sin = sin_ref[...]
    nope, rope, vd = QK_NOPE_DIM, QK_ROPE_DIM, V_HEAD_DIM

    lru_ref[...] = _dot(xb, win_ref[:, 0:lru_w])
    gate_ref[...] = jax.nn.gelu(_dot(xb, win_ref[:, lru_w:2 * lru_w])).astype(BF16)

    o = 2 * lru_w
    c_q = _dot(xb, win_ref[:, o:o + q_rank])
    o += q_rank
    c_kv = _dot(xb, win_ref[:, o:o + kv_rank])
    o += kv_rank
    kpe = _dot(xb, win_ref[:, o:o + 2 * rope])
    k_rot = kpe[:, :rope] * cos + kpe[:, rope:] * sin

    qn = _rms_norm(c_q, qg_ref[...]).astype(BF16)
    kvn = _rms_norm(c_kv, kvg_ref[...]).astype(BF16)
    qscale = (nope + rope) ** -0.5 * LOG2E
    hq = nope + 2 * rope
    for h in range(MLA_HEADS):
        qh = _dot(qn, wq_ref[:, h * hq:(h + 1) * hq])
        q_rot = qh[:, nope:nope + rope] * cos + qh[:, nope + rope:] * sin
        q_ref[0, h] = (jnp.concatenate([qh[:, :nope], q_rot], -1) * qscale).astype(BF16)
        kvh = _dot(kvn, wkv_ref[:, h * (nope + vd):(h + 1) * (nope + vd)])
        k_ref[0, h] = jnp.concatenate([kvh[:, :nope], k_rot], -1).astype(BF16)
        v_ref[0, h] = kvh[:, nope:].astype(BF16)


def in_proj(x, w_in, w_q, w_kv, q_g, kv_g, cos64, sin64, *, batch, tm, lru_w):
    n, d = x.shape
    s = n // batch
    spb = s // tm
    q_rank, kv_rank = w_q.shape[0], w_kv.shape[0]
    qk = QK_NOPE_DIM + QK_ROPE_DIM
    const = lambda i: (0, 0)
    head_map = lambda i: (i // spb, 0, i % spb, 0)
    kern = functools.partial(_in_proj_kernel, lru_w=lru_w, q_rank=q_rank, kv_rank=kv_rank)
    return pl.pallas_call(
        kern,
        grid=(n // tm,),
        in_specs=[pl.BlockSpec((tm, d), lambda i: (i, 0)),
                  pl.BlockSpec(w_in.shape, const),
                  pl.BlockSpec(w_q.shape, const),
                  pl.BlockSpec(w_kv.shape, const),
                  pl.BlockSpec((1, q_rank), const),
                  pl.BlockSpec((1, kv_rank), const),
                  pl.BlockSpec((tm, QK_ROPE_DIM), lambda i: (i, 0)),
                  pl.BlockSpec((tm, QK_ROPE_DIM), lambda i: (i, 0))],
        out_specs=[pl.BlockSpec((tm, lru_w), lambda i: (i, 0)),
                   pl.BlockSpec((tm, lru_w), lambda i: (i, 0)),
                   pl.BlockSpec((1, MLA_HEADS, tm, qk), head_map),
                   pl.BlockSpec((1, MLA_HEADS, tm, qk), head_map),
                   pl.BlockSpec((1, MLA_HEADS, tm, V_HEAD_DIM), head_map)],
        out_shape=[jax.ShapeDtypeStruct((n, lru_w), F32),
                   jax.ShapeDtypeStruct((n, lru_w), BF16),
                   jax.ShapeDtypeStruct((batch, MLA_HEADS, s, qk), BF16),
                   jax.ShapeDtypeStruct((batch, MLA_HEADS, s, qk), BF16),
                   jax.ShapeDtypeStruct((batch, MLA_HEADS, s, V_HEAD_DIM), BF16)],
        compiler_params=_params("parallel"),
        name="in_proj",
    )(x, w_in, w_q, w_kv, q_g, kv_g, cos64, sin64)


def _rglru_kernel(x_ref, gate_ref, cw_ref, cb_ref, wa_ref, ba_ref, wx_ref, bx_ref, lam_ref,
                  y_ref, xbuf, a_scr, u_scr, h_scr, *, ts, pad):
    t = pl.program_id(1)
    w = x_ref.shape[-1]

    @pl.when(t == 0)
    def _():
        xbuf[0:pad, :] = jnp.zeros((pad, w), F32)
        h_scr[0:1, :] = jnp.zeros((1, w), F32)

    xbuf[pad:pad + ts, :] = x_ref[0]
    xc = cb_ref[...] + cw_ref[CONV_WIDTH - 1:CONV_WIDTH, :] * xbuf[pad:pad + ts, :]
    for k in range(CONV_WIDTH - 1):
        sh = CONV_WIDTH - 1 - k
        xc = xc + cw_ref[k:k + 1, :] * xbuf[pad - sh:pad - sh + ts, :]
    xbuf[0:pad, :] = xbuf[ts:ts + pad, :]

    xcb = xc.astype(BF16)
    ngrp = w // MXU_DIM
    rec = jnp.concatenate(
        [_dot(xcb[:, g * MXU_DIM:(g + 1) * MXU_DIM], wa_ref[g]) for g in range(ngrp)], -1)
    ing = jnp.concatenate(
        [_dot(xcb[:, g * MXU_DIM:(g + 1) * MXU_DIM], wx_ref[g]) for g in range(ngrp)], -1)
    rec = jax.nn.sigmoid(rec + ba_ref[...])
    ing = jax.nn.sigmoid(ing + bx_ref[...])

    nlam = -lam_ref[...]
    softplus = jnp.maximum(nlam, 0.0) + jnp.log1p(jnp.exp(-jnp.abs(nlam)))
    log_a = -LRU_C * rec * softplus
    a = jnp.exp(log_a)
    mult = jnp.sqrt(jnp.tanh(-log_a) * (1.0 + a * a))
    a_scr[...] = a
    u_scr[...] = mult * (ing * xc)

    def step(i, h):
        h = a_scr[pl.ds(i, 1), :] * h + u_scr[pl.ds(i, 1), :]
        u_scr[pl.ds(i, 1), :] = h
        return h

    h_last = lax.fori_loop(0, ts, step, h_scr[0:1, :], unroll=8)
    h_scr[0:1, :] = h_last
    y_ref[0] = (gate_ref[0].astype(F32) * u_scr[...]).astype(BF16)


def rglru(lru_in, gate, conv_w, conv_b, wa_bd, b_a, wx_bd, b_x, lam, *, ts):
    bsz, s, w = lru_in.shape
    pad = 8
    const2 = lambda b, t: (0, 0)
    const3 = lambda b, t: (0, 0, 0)
    kern = functools.partial(_rglru_kernel, ts=ts, pad=pad)
    return pl.pallas_call(
        kern,
        grid=(bsz, s // ts),
        in_specs=[pl.BlockSpec((1, ts, w), lambda b, t: (b, t, 0)),
                  pl.BlockSpec((1, ts, w), lambda b, t: (b, t, 0)),
                  pl.BlockSpec(conv_w.shape, const2),
                  pl.BlockSpec((1, w), const2),
                  pl.BlockSpec(wa_bd.shape, const3),
                  pl.BlockSpec((1, w), const2),
                  pl.BlockSpec(wx_bd.shape, const3),
                  pl.BlockSpec((1, w), const2),
                  pl.BlockSpec((1, w), const2)],
        out_specs=pl.BlockSpec((1, ts, w), lambda b, t: (b, t, 0)),
        out_shape=jax.ShapeDtypeStruct((bsz, s, w), BF16),
        scratch_shapes=[pltpu.VMEM((ts + pad, w), F32),
                        pltpu.VMEM((ts, w), F32),
                        pltpu.VMEM((ts, w), F32),
                        pltpu.VMEM((8, w), F32)],
        compiler_params=_params("parallel", "arbitrary"),
        name="rglru",
    )(lru_in, gate, conv_w, conv_b, wa_bd, b_a, wx_bd, b_x, lam)


def _block_diag_groups(wblk):
    g, d, _ = wblk.shape
    per = MXU_DIM // d
    wg = wblk.reshape(g // per, per, d, d)
    eye = jnp.eye(per, dtype=wblk.dtype)
    out = jnp.einsum('gpij,pq->gpiqj', wg, eye)
    return out.reshape(g // per, MXU_DIM, MXU_DIM)


def _attn_kernel(q_ref, k_ref, v_ref, o_ref, *, tq):
    s = q_ref.shape[0]
    nq = s // tq
    dims = (((1,), (1,)), ((), ()))

    def tile(q, kj, carry, masked):
        m, l, acc = carry
        k = k_ref[pl.ds(kj * tq, tq), :]
        v = v_ref[pl.ds(kj * tq, tq), :]
        sc = lax.dot_general(q, k, dims, preferred_element_type=F32)
        if masked:
            row = lax.broadcasted_iota(jnp.int32, (tq, tq), 0)
            col = lax.broadcasted_iota(jnp.int32, (tq, tq), 1)
            sc = jnp.where(col <= row, sc, -jnp.inf)
        m_new = jnp.maximum(m, jnp.max(sc, axis=-1, keepdims=True))
        p = jnp.exp2(sc - m_new)
        alpha = jnp.exp2(m - m_new)
        l = alpha * l + jnp.sum(p, axis=-1, keepdims=True)
        acc = alpha * acc + _dot(p.astype(BF16), v)
        return m_new, l, acc

    def q_tile(qi, _):
        q = q_ref[pl.ds(qi * tq, tq), :]
        init = (jnp.full((tq, 1), -jnp.inf, F32), jnp.zeros((tq, 1), F32),
                jnp.zeros((tq, v_ref.shape[-1]), F32))
        carry = lax.fori_loop(0, qi, lambda kj, c: tile(q, kj, c, False), init)
        m, l, acc = tile(q, qi, carry, True)
        o_ref[pl.ds(qi * tq, tq), :] = (acc / l).astype(o_ref.dtype)
        return 0

    lax.fori_loop(0, nq, q_tile, 0)


def attention(q, k, v, *, tq):
    bsz, nh, s, dqk = q.shape
    dv = v.shape[-1]
    return pl.pallas_call(
        functools.partial(_attn_kernel, tq=tq),
        grid=(bsz, nh),
        in_specs=[pl.BlockSpec((None, None, s, dqk), lambda b, h: (b, h, 0, 0)),
                  pl.BlockSpec((None, None, s, dqk), lambda b, h: (b, h, 0, 0)),
                  pl.BlockSpec((None, None, s, dv), lambda b, h: (b, h, 0, 0))],
        out_specs=pl.BlockSpec((None, s, dv), lambda b, h: (b, 0, h)),
        out_shape=jax.ShapeDtypeStruct((bsz, s, nh * dv), BF16),
        compiler_params=_params("parallel", "parallel"),
        name="mla_attention",
    )(q, k, v)


def _out_proj_kernel(x_ref, yl_ref, ym_ref, wl_ref, wm_ref, g_ref, b_ref, o_ref):
    mix = _dot(yl_ref[...], wl_ref[...]) + _dot(ym_ref[...], wm_ref[...])
    o_ref[...] = _layer_norm(DEEPNORM_ALPHA * x_ref[...] + mix, g_ref[...], b_ref[...])


def out_proj_ln(x, y_lru, y_mla, w_lru, w_mla, g, b, *, tm):
    n, d = x.shape
    wl, wm = y_lru.shape[1], y_mla.shape[1]
    const = lambda i: (0, 0)
    return pl.pallas_call(
        _out_proj_kernel,
        grid=(n // tm,),
        in_specs=[pl.BlockSpec((tm, d), lambda i: (i, 0)),
                  pl.BlockSpec((tm, wl), lambda i: (i, 0)),
                  pl.BlockSpec((tm, wm), lambda i: (i, 0)),
                  pl.BlockSpec((wl, d), const),
                  pl.BlockSpec((wm, d), const),
                  pl.BlockSpec((1, d), const),
                  pl.BlockSpec((1, d), const)],
        out_specs=pl.BlockSpec((tm, d), lambda i: (i, 0)),
        out_shape=jax.ShapeDtypeStruct((n, d), F32),
        compiler_params=_params("parallel"),
        name="out_proj_ln",
    )(x, y_lru, y_mla, w_lru, w_mla, g, b)


def _ple_kernel(x_ref, p_ref, wg_ref, bg_ref, wp_ref, g_ref, b_ref, o_ref):
    x = x_ref[...]
    gate = jax.nn.sigmoid(_dot(x.astype(BF16), wg_ref[...]) + bg_ref[...])
    proj = _dot(p_ref[...].astype(BF16), wp_ref[...])
    o_ref[...] = _layer_norm(DEEPNORM_ALPHA * x + gate * proj, g_ref[...], b_ref[...])


def ple_ln(x, p, w_gate, b_gate, w_proj, g, b, *, tm):
    n, d = x.shape
    pd = p.shape[1]
    const = lambda i: (0, 0)
    return pl.pallas_call(
        _ple_kernel,
        grid=(n // tm,),
        in_specs=[pl.BlockSpec((tm, d), lambda i: (i, 0)),
                  pl.BlockSpec((tm, pd), lambda i: (i, 0)),
                  pl.BlockSpec((d, d), const),
                  pl.BlockSpec((1, d), const),
                  pl.BlockSpec((pd, d), const),
                  pl.BlockSpec((1, d), const),
                  pl.BlockSpec((1, d), const)],
        out_specs=pl.BlockSpec((tm, d), lambda i: (i, 0)),
        out_shape=jax.ShapeDtypeStruct((n, d), F32),
        compiler_params=_params("parallel"),
        name="ple_ln",
    )(x, p, w_gate, b_gate, w_proj, g, b)


def _swap_halves_cols(w):
    half = w.shape[-1] // 2
    return jnp.concatenate([w[..., half:], w[..., :half]], -1)


def _tile(n, pref):
    t = min(n, pref)
    assert n % t == 0
    return t


def _layer(x, p, cos64, sin64, prm, *, batch):
    n, d = x.shape
    row = lambda v: v.reshape(1, -1)
    tm = _tile(n // batch, 512)

    x = ffn_ln(x, prm["ffn1_w_gate"].astype(BF16), prm["ffn1_w_up"].astype(BF16),
               prm["ffn1_w_down"].astype(BF16), row(prm["ln1_g"]), row(prm["ln1_b"]),
               tm=tm, tf=_tile(prm["ffn1_w_gate"].shape[1], 512))

    lru_w = prm["conv_w"].shape[1]
    w_in = prm["w_in"]
    w_kpe = w_in[:, -QK_ROPE_DIM:]
    w_in_ext = jnp.concatenate([w_in, _swap_halves_cols(w_kpe)], -1).astype(BF16)
    q_rank = prm["w_q_up"].shape[0]
    wq = prm["w_q_up"].reshape(q_rank, MLA_HEADS, QK_NOPE_DIM + QK_ROPE_DIM)
    wq_ext = jnp.concatenate([wq, _swap_halves_cols(wq[..., QK_NOPE_DIM:])], -1)
    wq_ext = wq_ext.reshape(q_rank, -1).astype(BF16)
    lru_in, gate, q, k, v = in_proj(
        x, w_in_ext, wq_ext, prm["w_kv_up"].astype(BF16), row(prm["q_norm_g"]),
        row(prm["kv_norm_g"]), cos64, sin64, batch=batch, tm=tm, lru_w=lru_w)

    s = n // batch
    y_lru = rglru(lru_in.reshape(batch, s, lru_w), gate.reshape(batch, s, lru_w),
                  prm["conv_w"], row(prm["conv_b"]),
                  _block_diag_groups(prm["lru_w_a"]).astype(BF16), row(prm["lru_b_a"]),
                  _block_diag_groups(prm["lru_w_x"]).astype(BF16), row(prm["lru_b_x"]),
                  row(prm["lru_lambda"]), ts=_tile(s, 512))
    y_mla = attention(q, k, v, tq=_tile(s, 512))

    w_out = prm["w_out"].astype(BF16)
    x = out_proj_ln(x, y_lru.reshape(n, lru_w), y_mla.reshape(n, -1), w_out[:lru_w], w_out[lru_w:],
                    row(prm["ln2_g"]), row(prm["ln2_b"]), tm=tm)

    x = ffn_ln(x, prm["ffn2_w_gate"].astype(BF16), prm["ffn2_w_up"].astype(BF16),
               prm["ffn2_w_down"].astype(BF16), row(prm["ln3_g"]), row(prm["ln3_b"]),
               tm=tm, tf=_tile(prm["ffn2_w_gate"].shape[1], 512))

    return ple_ln(x, p, prm["ple_w_gate"].astype(BF16), row(prm["ple_b_gate"]),
                  prm["ple_w_proj"].astype(BF16), row(prm["ln4_g"]), row(prm["ln4_b"]), tm=tm)


def kernel(x, p, positions, ffn1_w_gate, ffn1_w_up, ffn1_w_down, ln1_g, ln1_b, w_in, conv_w, conv_b, lru_w_a, lru_b_a, lru_w_x, lru_b_x, lru_lambda, q_norm_g, w_q_up, kv_norm_g, w_kv_up, w_out, ln2_g, ln2_b, ffn2_w_gate, ffn2_w_up, ffn2_w_down, ln3_g, ln3_b, ple_w_gate, ple_b_gate, ple_w_proj, ln4_g, ln4_b):
    names = ("ffn1_w_gate ffn1_w_up ffn1_w_down ln1_g ln1_b w_in conv_w conv_b lru_w_a lru_b_a lru_w_x "
             "lru_b_x lru_lambda q_norm_g w_q_up kv_norm_g w_kv_up w_out ln2_g ln2_b ffn2_w_gate ffn2_w_up "
             "ffn2_w_down ln3_g ln3_b ple_w_gate ple_b_gate ple_w_proj ln4_g ln4_b").split()
    stacked = dict(zip(names, (ffn1_w_gate, ffn1_w_up, ffn1_w_down, ln1_g, ln1_b, w_in, conv_w, conv_b,
                               lru_w_a, lru_b_a, lru_w_x, lru_b_x, lru_lambda, q_norm_g, w_q_up, kv_norm_g,
                               w_kv_up, w_out, ln2_g, ln2_b, ffn2_w_gate, ffn2_w_up, ffn2_w_down, ln3_g,
                               ln3_b, ple_w_gate, ple_b_gate, ple_w_proj, ln4_g, ln4_b)))
    batch, seq, d = x.shape
    cos64, sin64 = rope_tables(positions)
    h = x.reshape(batch * seq, d)
    assert ffn1_w_gate.shape[0] == DEPTH
    for i in range(DEPTH):
        prm = {kname: val[i] for kname, val in stacked.items()}
        h = _layer(h, p[i].reshape(batch * seq, -1), cos64, sin64, prm, batch=batch)
    return h.reshape(batch, seq, d)
```

```python
import functools
import math

import jax
import jax.numpy as jnp
import numpy as np
from jax import lax
from jax.experimental import pallas as pl
from jax.experimental.pallas import tpu as pltpu

LRU_BLOCKS = 16
CONV_WIDTH = 4
LRU_C = 8.0
MLA_HEADS = 8
QK_NOPE_DIM = 128
QK_ROPE_DIM = 64
V_HEAD_DIM = 128
ROPE_THETA = 10000.0
LN_EPS = 1e-5
RMS_EPS = 1e-6
DEPTH = 1
DEEPNORM_ALPHA = (2 * DEPTH) ** 0.25

LANES = 128
MXU_DIM = 256
VMEM_LIMIT = 56 * 1024 * 1024
LOG2E = math.log2(math.e)

BF16 = jnp.bfloat16
F32 = jnp.float32


def _params(*sem):
    return pltpu.CompilerParams(dimension_semantics=sem, vmem_limit_bytes=VMEM_LIMIT)


def _layer_norm(y, g, b):
    mu = jnp.mean(y, axis=-1, keepdims=True)
    yc = y - mu
    var = jnp.mean(yc * yc, axis=-1, keepdims=True)
    return yc * lax.rsqrt(var + LN_EPS) * g + b


def _rms_norm(y, g):
    return y * lax.rsqrt(jnp.mean(y * y, axis=-1, keepdims=True) + RMS_EPS) * g


def _dot(a, b):
    return jnp.dot(a, b, preferred_element_type=F32)


def _rope_kernel(inv_ref, pos_ref, cos_ref, sin_ref):
    j = pl.program_id(0)
    ang = pos_ref[...].astype(F32) * inv_ref[j]
    cos_ref[0] = jnp.cos(ang)
    sin_ref[0] = jnp.sin(ang)


def rope_tables(positions):
    n = positions.size
    half = QK_ROPE_DIM // 2
    inv_freq = ROPE_THETA ** (-jnp.arange(0, QK_ROPE_DIM, 2, dtype=F32) / QK_ROPE_DIM)
    pos2 = positions.reshape(n // LANES, LANES)
    shp = jax.ShapeDtypeStruct((half, n // LANES, LANES), F32)
    cos, sin = pl.pallas_call(
        _rope_kernel,
        grid=(half,),
        in_specs=[pl.BlockSpec(memory_space=pltpu.SMEM),
                  pl.BlockSpec((n // LANES, LANES), lambda j: (0, 0))],
        out_specs=[pl.BlockSpec((1, n // LANES, LANES), lambda j: (j, 0, 0))] * 2,
        out_shape=[shp, shp],
        compiler_params=_params("arbitrary"),
        name="rope_tables",
    )(inv_freq, pos2)
    cos = cos.reshape(half, n).T
    sin = sin.reshape(half, n).T
    return jnp.concatenate([cos, cos], -1), jnp.concatenate([-sin, sin], -1)


def _ffn_ln_kernel(x_ref, wg_ref, wu_ref, wd_ref, g_ref, b_ref, o_ref, xb_ref):
    j = pl.program_id(1)

    @pl.when(j == 0)
    def _():
        xb_ref[...] = x_ref[...].astype(BF16)
        o_ref[...] = jnp.zeros_like(o_ref)

    xb = xb_ref[...]
    gate = _dot(xb, wg_ref[...])
    up = _dot(xb, wu_ref[...])
    h = (gate * jax.nn.sigmoid(gate) * up).astype(BF16)
    o_ref[...] += _dot(h, wd_ref[...])

    @pl.when(j == pl.num_programs(1) - 1)
    def _():
        y = DEEPNORM_ALPHA * x_ref[...] + 0.5 * o_ref[...]
        o_ref[...] = _layer_norm(y, g_ref[...], b_ref[...])


def ffn_ln(x, wg, wu, wd, g, b, *, tm, tf):
    n, d = x.shape
    f = wg.shape[1]
    return pl.pallas_call(
        _ffn_ln_kernel,
        grid=(n // tm, f // tf),
        in_specs=[pl.BlockSpec((tm, d), lambda i, j: (i, 0)),
                  pl.BlockSpec((d, tf), lambda i, j: (0, j)),
                  pl.BlockSpec((d, tf), lambda i, j: (0, j)),
                  pl.BlockSpec((tf, d), lambda i, j: (j, 0)),
                  pl.BlockSpec((1, d), lambda i, j: (0, 0)),
                  pl.BlockSpec((1, d), lambda i, j: (0, 0))],
        out_specs=pl.BlockSpec((tm, d), lambda i, j: (i, 0)),
        out_shape=jax.ShapeDtypeStruct((n, d), F32),
        scratch_shapes=[pltpu.VMEM((tm, d), BF16)],
        compiler_params=_params("parallel", "arbitrary"),
        name="ffn_ln",
    )(x, wg, wu, wd, g, b)


def _in_proj_kernel(x_ref, win_ref, wq_ref, wkv_ref, qg_ref, kvg_ref, cos_ref, sin_ref,
                    lru_ref, gate_ref, q_ref, k_ref, v_ref, *, lru_w, q_rank, kv_rank):
    xb = x_ref[...].astype(BF16)
    cos = cos_ref[...]
    sin = sin_ref[...]
    nope, rope, vd = QK_NOPE_DIM, QK_ROPE_DIM, V_HEAD_DIM

    lru_ref[...] = _dot(xb, win_ref[:, 0:lru_w])
    gate_ref[...] = jax.nn.gelu(_dot(xb, win_ref[:, lru_w:2 * lru_w])).astype(BF16)

    o = 2 * lru_w
    c_q = _dot(xb, win_ref[:, o:o + q_rank])
    o += q_rank
    c_kv = _dot(xb, win_ref[:, o:o + kv_rank])
    o += kv_rank
    kpe = _dot(xb, win_ref[:, o:o + 2 * rope])
    k_rot = kpe[:, :rope] * cos + kpe[:, rope:] * sin

    qn = _rms_norm(c_q, qg_ref[...]).astype(BF16)
    kvn = _rms_norm(c_kv, kvg_ref[...]).astype(BF16)
    qscale = (nope + rope) ** -0.5 * LOG2E
    hq = nope + 2 * rope
    for h in range(MLA_HEADS):
        qh = _dot(qn, wq_ref[:, h * hq:(h + 1) * hq])
        q_rot = qh[:, nope:nope + rope] * cos + qh[:, nope + rope:] * sin
        q_ref[0, h] = (jnp.concatenate([qh[:, :nope], q_rot], -1) * qscale).astype(BF16)
        kvh = _dot(kvn, wkv_ref[:, h * (nope + vd):(h + 1) * (nope + vd)])
        k_ref[0, h] = jnp.concatenate([kvh[:, :nope], k_rot], -1).astype(BF16)
        vt = kvh[:, nope:].T.astype(BF16)
        tk = v_ref.shape[-1]
        for c in range(v_ref.shape[2]):
            v_ref[0, h, c, 0:vd] = vt[:, c * tk:(c + 1) * tk]
            v_ref[0, h, c, vd:] = jnp.ones((v_ref.shape[3] - vd, tk), BF16)


def in_proj(x, w_in, w_q, w_kv, q_g, kv_g, cos64, sin64, *, batch, tm, lru_w):
    n, d = x.shape
    s = n // batch
    spb = s // tm
    q_rank, kv_rank = w_q.shape[0], w_kv.shape[0]
    qk = QK_NOPE_DIM + QK_ROPE_DIM
    tk = min(ATTN_TK, tm)
    const = lambda i: (0, 0)
    head_map = lambda i: (i // spb, 0, i % spb, 0)
    kern = functools.partial(_in_proj_kernel, lru_w=lru_w, q_rank=q_rank, kv_rank=kv_rank)
    return pl.pallas_call(
        kern,
        grid=(n // tm,),
        in_specs=[pl.BlockSpec((tm, d), lambda i: (i, 0)),
                  pl.BlockSpec(w_in.shape, const),
                  pl.BlockSpec(w_q.shape, const),
                  pl.BlockSpec(w_kv.shape, const),
                  pl.BlockSpec((1, q_rank), const),
                  pl.BlockSpec((1, kv_rank), const),
                  pl.BlockSpec((tm, QK_ROPE_DIM), lambda i: (i, 0)),
                  pl.BlockSpec((tm, QK_ROPE_DIM), lambda i: (i, 0))],
        out_specs=[pl.BlockSpec((tm, lru_w), lambda i: (i, 0)),
                   pl.BlockSpec((tm, lru_w), lambda i: (i, 0)),
                   pl.BlockSpec((1, MLA_HEADS, tm, qk), head_map),
                   pl.BlockSpec((1, MLA_HEADS, tm, qk), head_map),
                   pl.BlockSpec((1, MLA_HEADS, tm // tk, V_HEAD_DIM + ATTN_ONES_ROWS, tk),
                                lambda i: (i // spb, 0, i % spb, 0, 0))],
        out_shape=[jax.ShapeDtypeStruct((n, lru_w), F32),
                   jax.ShapeDtypeStruct((n, lru_w), BF16),
                   jax.ShapeDtypeStruct((batch, MLA_HEADS, s, qk), BF16),
                   jax.ShapeDtypeStruct((batch, MLA_HEADS, s, qk), BF16),
                   jax.ShapeDtypeStruct((batch, MLA_HEADS, s // tk, V_HEAD_DIM + ATTN_ONES_ROWS, tk), BF16)],
        compiler_params=_params("parallel"),
        name="in_proj",
    )(x, w_in, w_q, w_kv, q_g, kv_g, cos64, sin64)


def _rglru_kernel(x_ref, gate_ref, cw_ref, cb_ref, wa_ref, ba_ref, wx_ref, bx_ref, lam_ref,
                  y_ref, xbuf, a_scr, u_scr, h_scr, *, ts, pad):
    t = pl.program_id(1)
    w = x_ref.shape[-1]

    @pl.when(t == 0)
    def _():
        xbuf[0:pad, :] = jnp.zeros((pad, w), F32)
        h_scr[0:1, :] = jnp.zeros((1, w), F32)

    xbuf[pad:pad + ts, :] = x_ref[0]
    xc = cb_ref[...] + cw_ref[CONV_WIDTH - 1:CONV_WIDTH, :] * xbuf[pad:pad + ts, :]
    for k in range(CONV_WIDTH - 1):
        sh = CONV_WIDTH - 1 - k
        xc = xc + cw_ref[k:k + 1, :] * xbuf[pad - sh:pad - sh + ts, :]
    xbuf[0:pad, :] = xbuf[ts:ts + pad, :]

    xcb = xc.astype(BF16)
    ngrp = w // MXU_DIM
    rec = jnp.concatenate(
        [_dot(xcb[:, g * MXU_DIM:(g + 1) * MXU_DIM], wa_ref[g]) for g in range(ngrp)], -1)
    ing = jnp.concatenate(
        [_dot(xcb[:, g * MXU_DIM:(g + 1) * MXU_DIM], wx_ref[g]) for g in range(ngrp)], -1)
    rec = jax.nn.sigmoid(rec + ba_ref[...])
    ing = jax.nn.sigmoid(ing + bx_ref[...])

    nlam = -lam_ref[...]
    softplus = jnp.maximum(nlam, 0.0) + jnp.log1p(jnp.exp(-jnp.abs(nlam)))
    log_a = -LRU_C * rec * softplus
    a = jnp.exp(log_a)
    mult = jnp.sqrt(jnp.tanh(-log_a) * (1.0 + a * a))
    a_scr[...] = a
    u_scr[...] = mult * (ing * xc)

    def step(i, h):
        h = a_scr[pl.ds(i, 1), :] * h + u_scr[pl.ds(i, 1), :]
        u_scr[pl.ds(i, 1), :] = h
        return h

    h_last = lax.fori_loop(0, ts, step, h_scr[0:1, :], unroll=8)
    h_scr[0:1, :] = h_last
    y_ref[0] = (gate_ref[0].astype(F32) * u_scr[...]).astype(BF16)


def rglru(lru_in, gate, conv_w, conv_b, wa_bd, b_a, wx_bd, b_x, lam, *, ts):
    bsz, s, w = lru_in.shape
    pad = 8
    const2 = lambda b, t: (0, 0)
    const3 = lambda b, t: (0, 0, 0)
    kern = functools.partial(_rglru_kernel, ts=ts, pad=pad)
    return pl.pallas_call(
        kern,
        grid=(bsz, s // ts),
        in_specs=[pl.BlockSpec((1, ts, w), lambda b, t: (b, t, 0)),
                  pl.BlockSpec((1, ts, w), lambda b, t: (b, t, 0)),
                  pl.BlockSpec(conv_w.shape, const2),
                  pl.BlockSpec((1, w), const2),
                  pl.BlockSpec(wa_bd.shape, const3),
                  pl.BlockSpec((1, w), const2),
                  pl.BlockSpec(wx_bd.shape, const3),
                  pl.BlockSpec((1, w), const2),
                  pl.BlockSpec((1, w), const2)],
        out_specs=pl.BlockSpec((1, ts, w), lambda b, t: (b, t, 0)),
        out_shape=jax.ShapeDtypeStruct((bsz, s, w), BF16),
        scratch_shapes=[pltpu.VMEM((ts + pad, w), F32),
                        pltpu.VMEM((ts, w), F32),
                        pltpu.VMEM((ts, w), F32),
                        pltpu.VMEM((8, w), F32)],
        compiler_params=_params("parallel", "arbitrary"),
        name="rglru",
    )(lru_in, gate, conv_w, conv_b, wa_bd, b_a, wx_bd, b_x, lam)


def _block_diag_groups(wblk):
    g, d, _ = wblk.shape
    per = MXU_DIM // d
    wg = wblk.reshape(g // per, per, d, d)
    eye = jnp.eye(per, dtype=wblk.dtype)
    out = jnp.einsum('gpij,pq->gpiqj', wg, eye)
    return out.reshape(g // per, MXU_DIM, MXU_DIM)


HEADS_PER_STEP = 2
ATTN_TK = 256
ATTN_ONES_ROWS = 16
ATTN_KV_UNROLL = 2


def _attn_kernel(q_ref, k_ref, vt_ref, o_ref, *scratch, tq):
    nh, s, _ = q_ref.shape
    dva, tk = vt_ref.shape[2], vt_ref.shape[3]
    dv = dva - ATTN_ONES_ROWS
    nq, sub = s // tq, tq // tk
    assert sub % 2 == 0
    dims = (((1,), (1,)), ((), ()))
    heads = range(nh)
    st_scr = [scratch[2 * hh:2 * hh + 2] for hh in heads]
    acc_scr = scratch[2 * nh:]

    def scores(qs, kblk, slot):
        for hh in heads:
            st_scr[hh][slot][...] = lax.dot_general(k_ref[hh, pl.ds(kblk * tk, tk), :], qs[hh], dims,
                                                    preferred_element_type=F32)

    def update(hh, kblk, slot, diag_off, m):
        st = st_scr[hh][slot][...]
        if diag_off is not None:
            key = lax.broadcasted_iota(jnp.int32, (tk, tq), 0) + diag_off
            qry = lax.broadcasted_iota(jnp.int32, (tk, tq), 1)
            st = jnp.where(key <= qry, st, -jnp.inf)
        m_new = jnp.maximum(m, jnp.max(st, axis=0, keepdims=True))
        pt = jnp.exp2(st - m_new).astype(BF16)
        acc_scr[hh][...] = (jnp.exp2(m - m_new) * acc_scr[hh][...]
                            + _dot(vt_ref[hh, kblk], pt))
        return m_new

    def block(qs, kblk, slot, diag_off, ms, lookahead):
        if lookahead:
            scores(qs, kblk + 1, 1 - slot)
        return tuple(update(hh, kblk, slot, diag_off, ms[hh]) for hh in heads)

    def q_tile(qi, _):
        qs = [q_ref[hh, pl.ds(qi * tq, tq), :] for hh in heads]
        for hh in heads:
            acc_scr[hh][...] = jnp.zeros((dva, tq), F32)
        scores(qs, 0, 0)

        def kv_tiles(ntiles, kj, ms):
            for c in range(ntiles * sub):
                ms = block(qs, kj * sub + c, c % 2, None, ms, True)
            return ms

        n_main = qi // ATTN_KV_UNROLL
        ms = tuple(jnp.full((1, tq), -jnp.inf, F32) for _ in heads)
        ms = lax.fori_loop(0, n_main,
                           lambda it, ms: kv_tiles(ATTN_KV_UNROLL, it * ATTN_KV_UNROLL, ms), ms)
        ms = lax.fori_loop(n_main * ATTN_KV_UNROLL, qi, functools.partial(kv_tiles, 1), ms)
        for c in range(sub):
            ms = block(qs, qi * sub + c, c % 2, c * tk, ms, c + 1 < sub)
        for hh in heads:
            out = acc_scr[hh][0:dv, :] / acc_scr[hh][dv:dv + 1, :]
            o_ref[pl.ds(qi * tq, tq), hh * dv:(hh + 1) * dv] = out.T.astype(o_ref.dtype)
        return 0

    lax.fori_loop(0, nq, q_tile, 0)


def attention(q, k, vt, *, tq):
    bsz, nh, s, dqk = q.shape
    _, _, nkb, dva, tk = vt.shape
    dv = dva - ATTN_ONES_ROWS
    hps = HEADS_PER_STEP
    return pl.pallas_call(
        functools.partial(_attn_kernel, tq=tq),
        grid=(bsz, nh // hps),
        in_specs=[pl.BlockSpec((None, hps, s, dqk), lambda b, h: (b, h, 0, 0)),
                  pl.BlockSpec((None, hps, s, dqk), lambda b, h: (b, h, 0, 0)),
                  pl.BlockSpec((None, hps, nkb, dva, tk), lambda b, h: (b, h, 0, 0, 0))],
        out_specs=pl.BlockSpec((None, s, hps * dv), lambda b, h: (b, 0, h)),
        out_shape=jax.ShapeDtypeStruct((bsz, s, nh * dv), BF16),
        scratch_shapes=([pltpu.VMEM((tk, tq), F32)] * (2 * hps) + [pltpu.VMEM((dva, tq), F32)] * hps),
        compiler_params=_params("parallel", "parallel"),
        name="mla_attention",
    )(q, k, vt)


def _out_proj_kernel(x_ref, yl_ref, ym_ref, wl_ref, wm_ref, g_ref, b_ref, o_ref):
    mix = _dot(yl_ref[...], wl_ref[...]) + _dot(ym_ref[...], wm_ref[...])
    o_ref[...] = _layer_norm(DEEPNORM_ALPHA * x_ref[...] + mix, g_ref[...], b_ref[...])


def out_proj_ln(x, y_lru, y_mla, w_lru, w_mla, g, b, *, tm):
    n, d = x.shape
    wl, wm = y_lru.shape[1], y_mla.shape[1]
    const = lambda i: (0, 0)
    return pl.pallas_call(
        _out_proj_kernel,
        grid=(n // tm,),
        in_specs=[pl.BlockSpec((tm, d), lambda i: (i, 0)),
                  pl.BlockSpec((tm, wl), lambda i: (i, 0)),
                  pl.BlockSpec((tm, wm), lambda i: (i, 0)),
                  pl.BlockSpec((wl, d), const),
                  pl.BlockSpec((wm, d), const),
                  pl.BlockSpec((1, d), const),
                  pl.BlockSpec((1, d), const)],
        out_specs=pl.BlockSpec((tm, d), lambda i: (i, 0)),
        out_shape=jax.ShapeDtypeStruct((n, d), F32),
        compiler_params=_params("parallel"),
        name="out_proj_ln",
    )(x, y_lru, y_mla, w_lru, w_mla, g, b)


def _ple_kernel(x_ref, p_ref, wg_ref, bg_ref, wp_ref, g_ref, b_ref, o_ref):
    x = x_ref[...]
    gate = jax.nn.sigmoid(_dot(x.astype(BF16), wg_ref[...]) + bg_ref[...])
    proj = _dot(p_ref[...].astype(BF16), wp_ref[...])
    o_ref[...] = _layer_norm(DEEPNORM_ALPHA * x + gate * proj, g_ref[...], b_ref[...])


def ple_ln(x, p, w_gate, b_gate, w_proj, g, b, *, tm):
    n, d = x.shape
    pd = p.shape[1]
    const = lambda i: (0, 0)
    return pl.pallas_call(
        _ple_kernel,
        grid=(n // tm,),
        in_specs=[pl.BlockSpec((tm, d), lambda i: (i, 0)),
                  pl.BlockSpec((tm, pd), lambda i: (i, 0)),
                  pl.BlockSpec((d, d), const),
                  pl.BlockSpec((1, d), const),
                  pl.BlockSpec((pd, d), const),
                  pl.BlockSpec((1, d), const),
                  pl.BlockSpec((1, d), const)],
        out_specs=pl.BlockSpec((tm, d), lambda i: (i, 0)),
        out_shape=jax.ShapeDtypeStruct((n, d), F32),
        compiler_params=_params("parallel"),
        name="ple_ln",
    )(x, p, w_gate, b_gate, w_proj, g, b)


def _swap_halves_cols(w):
    half = w.shape[-1] // 2
    return jnp.concatenate([w[..., half:], w[..., :half]], -1)


def _tile(n, pref):
    t = min(n, pref)
    assert n % t == 0
    return t


def _layer(x, p, cos64, sin64, prm, *, batch):
    n, d = x.shape
    row = lambda v: v.reshape(1, -1)
    tm = _tile(n // batch, 512)

    x = ffn_ln(x, prm["ffn1_w_gate"].astype(BF16), prm["ffn1_w_up"].astype(BF16),
               prm["ffn1_w_down"].astype(BF16), row(prm["ln1_g"]), row(prm["ln1_b"]),
               tm=tm, tf=_tile(prm["ffn1_w_gate"].shape[1], 512))

    lru_w = prm["conv_w"].shape[1]
    w_in = prm["w_in"]
    w_kpe = w_in[:, -QK_ROPE_DIM:]
    w_in_ext = jnp.concatenate([w_in, _swap_halves_cols(w_kpe)], -1).astype(BF16)
    q_rank = prm["w_q_up"].shape[0]
    wq = prm["w_q_up"].reshape(q_rank, MLA_HEADS, QK_NOPE_DIM + QK_ROPE_DIM)
    wq_ext = jnp.concatenate([wq, _swap_halves_cols(wq[..., QK_NOPE_DIM:])], -1)
    wq_ext = wq_ext.reshape(q_rank, -1).astype(BF16)
    lru_in, gate, q, k, vt = in_proj(
        x, w_in_ext, wq_ext, prm["w_kv_up"].astype(BF16), row(prm["q_norm_g"]),
        row(prm["kv_norm_g"]), cos64, sin64, batch=batch, tm=tm, lru_w=lru_w)

    s = n // batch
    y_lru = rglru(lru_in.reshape(batch, s, lru_w), gate.reshape(batch, s, lru_w),
                  prm["conv_w"], row(prm["conv_b"]),
                  _block_diag_groups(prm["lru_w_a"]).astype(BF16), row(prm["lru_b_a"]),
                  _block_diag_groups(prm["lru_w_x"]).astype(BF16), row(prm["lru_b_x"]),
                  row(prm["lru_lambda"]), ts=_tile(s, 512))
    y_mla = attention(q, k, vt, tq=tm)

    w_out = prm["w_out"].astype(BF16)
    x = out_proj_ln(x, y_lru.reshape(n, lru_w), y_mla.reshape(n, -1), w_out[:lru_w], w_out[lru_w:],
                    row(prm["ln2_g"]), row(prm["ln2_b"]), tm=tm)

    x = ffn_ln(x, prm["ffn2_w_gate"].astype(BF16), prm["ffn2_w_up"].astype(BF16),
               prm["ffn2_w_down"].astype(BF16), row(prm["ln3_g"]), row(prm["ln3_b"]),
               tm=tm, tf=_tile(prm["ffn2_w_gate"].shape[1], 512))

    return ple_ln(x, p, prm["ple_w_gate"].astype(BF16), row(prm["ple_b_gate"]),
                  prm["ple_w_proj"].astype(BF16), row(prm["ln4_g"]), row(prm["ln4_b"]), tm=tm)


def kernel(x, p, positions, ffn1_w_gate, ffn1_w_up, ffn1_w_down, ln1_g, ln1_b, w_in, conv_w, conv_b, lru_w_a, lru_b_a, lru_w_x, lru_b_x, lru_lambda, q_norm_g, w_q_up, kv_norm_g, w_kv_up, w_out, ln2_g, ln2_b, ffn2_w_gate, ffn2_w_up, ffn2_w_down, ln3_g, ln3_b, ple_w_gate, ple_b_gate, ple_w_proj, ln4_g, ln4_b):
    names = ("ffn1_w_gate ffn1_w_up ffn1_w_down ln1_g ln1_b w_in conv_w conv_b lru_w_a lru_b_a lru_w_x "
             "lru_b_x lru_lambda q_norm_g w_q_up kv_norm_g w_kv_up w_out ln2_g ln2_b ffn2_w_gate ffn2_w_up "
             "ffn2_w_down ln3_g ln3_b ple_w_gate ple_b_gate ple_w_proj ln4_g ln4_b").split()
    stacked = dict(zip(names, (ffn1_w_gate, ffn1_w_up, ffn1_w_down, ln1_g, ln1_b, w_in, conv_w, conv_b,
                               lru_w_a, lru_b_a, lru_w_x, lru_b_x, lru_lambda, q_norm_g, w_q_up, kv_norm_g,
                               w_kv_up, w_out, ln2_g, ln2_b, ffn2_w_gate, ffn2_w_up, ffn2_w_down, ln3_g,
                               ln3_b, ple_w_gate, ple_b_gate, ple_w_proj, ln4_g, ln4_b)))
    batch, seq, d = x.shape
    cos64, sin64 = rope_tables(positions)
    h = x.reshape(batch * seq, d)
    assert ffn1_w_gate.shape[0] == DEPTH
    for i in range(DEPTH):
        prm = {kname: val[i] for kname, val in stacked.items()}
        h = _layer(h, p[i].reshape(batch * seq, -1), cos64, sin64, prm, batch=batch)
    return h.reshape(batch, seq, d)
```

```python
import functools
import math

import jax
import jax.numpy as jnp
import numpy as np
from jax import lax
from jax.experimental import pallas as pl
from jax.experimental.pallas import tpu as pltpu

LRU_BLOCKS = 16
CONV_WIDTH = 4
LRU_C = 8.0
MLA_HEADS = 8
QK_NOPE_DIM = 128
QK_ROPE_DIM = 64
V_HEAD_DIM = 128
ROPE_THETA = 10000.0
LN_EPS = 1e-5
RMS_EPS = 1e-6
DEPTH = 1
DEEPNORM_ALPHA = (2 * DEPTH) ** 0.25

LANES = 128
MXU_DIM = 256
VMEM_LIMIT = 61 * 1024 * 1024
LOG2E = math.log2(math.e)
LN_ROWS = 128
MM_ROWS = 256

BF16 = jnp.bfloat16
F32 = jnp.float32


def _params(*sem):
    return pltpu.CompilerParams(dimension_semantics=sem, vmem_limit_bytes=VMEM_LIMIT)


def _layer_norm(y, g, b):
    mu = jnp.mean(y, axis=-1, keepdims=True)
    yc = y - mu
    var = jnp.mean(yc * yc, axis=-1, keepdims=True)
    return yc * lax.rsqrt(var + LN_EPS) * g + b


def _rms_norm(y, g):
    return y * lax.rsqrt(jnp.mean(y * y, axis=-1, keepdims=True) + RMS_EPS) * g


def _dot(a, b):
    return jnp.dot(a, b, preferred_element_type=F32)


def _rope_kernel(inv_ref, pos_ref, tab_ref):
    half = inv_ref.shape[0]
    ang = pos_ref[...].astype(F32) * inv_ref[...]
    cos = jnp.cos(ang)
    sin = jnp.sin(ang)
    tab_ref[0 * half:1 * half, :] = cos
    tab_ref[1 * half:2 * half, :] = cos
    tab_ref[2 * half:3 * half, :] = -sin
    tab_ref[3 * half:4 * half, :] = sin


def rope_table(positions, *, tn):
    n = positions.size
    half = QK_ROPE_DIM // 2
    inv_freq = ROPE_THETA ** (-jnp.arange(0, QK_ROPE_DIM, 2, dtype=F32) / QK_ROPE_DIM)
    return pl.pallas_call(
        _rope_kernel,
        grid=(n // tn,),
        in_specs=[pl.BlockSpec((half, 1), lambda i: (0, 0)),
                  pl.BlockSpec((1, tn), lambda i: (0, i))],
        out_specs=pl.BlockSpec((4 * half, tn), lambda i: (0, i)),
        out_shape=jax.ShapeDtypeStruct((4 * half, n), F32),
        compiler_params=_params("parallel"),
        name="rope_table",
    )(inv_freq.reshape(half, 1), positions.reshape(1, n))


def _ffn_ln_kernel(x_ref, wg_ref, wu_ref, wd_ref, g_ref, b_ref, o_ref, xb_ref):
    j = pl.program_id(1)

    @pl.when(j == 0)
    def _():
        xb_ref[...] = x_ref[...].astype(BF16)
        o_ref[...] = jnp.zeros_like(o_ref)

    xb = xb_ref[...]
    gate = _dot(xb, wg_ref[...])
    up = _dot(xb, wu_ref[...])
    h = (gate * jax.nn.sigmoid(gate) * up).astype(BF16)
    o_ref[...] += _dot(h, wd_ref[...])

    @pl.when(j == pl.num_programs(1) - 1)
    def _():
        def rows(c, _):
            r = pl.ds(pl.multiple_of(c * LN_ROWS, LN_ROWS), LN_ROWS)
            y = DEEPNORM_ALPHA * x_ref[r, :] + 0.5 * o_ref[r, :]
            o_ref[r, :] = _layer_norm(y, g_ref[...], b_ref[...])
            return 0

        lax.fori_loop(0, o_ref.shape[0] // LN_ROWS, rows, 0)


def ffn_ln(x, wg, wu, wd, g, b, *, tm, tf):
    n, d = x.shape
    f = wg.shape[1]
    return pl.pallas_call(
        _ffn_ln_kernel,
        grid=(n // tm, f // tf),
        in_specs=[pl.BlockSpec((tm, d), lambda i, j: (i, 0)),
                  pl.BlockSpec((d, tf), lambda i, j: (0, j)),
                  pl.BlockSpec((d, tf), lambda i, j: (0, j)),
                  pl.BlockSpec((tf, d), lambda i, j: (j, 0)),
                  pl.BlockSpec((1, d), lambda i, j: (0, 0)),
                  pl.BlockSpec((1, d), lambda i, j: (0, 0))],
        out_specs=pl.BlockSpec((tm, d), lambda i, j: (i, 0)),
        out_shape=jax.ShapeDtypeStruct((n, d), F32),
        scratch_shapes=[pltpu.VMEM((tm, d), BF16)],
        compiler_params=_params("parallel", "arbitrary"),
        name="ffn_ln",
    )(x, wg, wu, wd, g, b)


def _in_proj_kernel(x_ref, win_ref, wq_ref, wkv_ref, qg_ref, kvg_ref, rope_ref,
                    lru_ref, gate_ref, q_ref, k_ref, v_ref, *, lru_w, q_rank, kv_rank):
    xb = x_ref[...].astype(BF16)
    nope, rope, vd = QK_NOPE_DIM, QK_ROPE_DIM, V_HEAD_DIM
    tab = rope_ref[...].T

    def rotate(pe_and_swapped):
        prod = pe_and_swapped * tab
        return prod[:, :rope] + prod[:, rope:]


    lru_ref[...] = _dot(xb, win_ref[:, 0:lru_w])
    gate_ref[...] = jax.nn.gelu(_dot(xb, win_ref[:, lru_w:2 * lru_w])).astype(BF16)

    o = 2 * lru_w
    c_q = _dot(xb, win_ref[:, o:o + q_rank])
    o += q_rank
    c_kv = _dot(xb, win_ref[:, o:o + kv_rank])
    o += kv_rank
    k_rot = rotate(_dot(xb, win_ref[:, o:o + 2 * rope]))

    qn = _rms_norm(c_q, qg_ref[...]).astype(BF16)
    kvn = _rms_norm(c_kv, kvg_ref[...]).astype(BF16)
    qscale = (nope + rope) ** -0.5 * LOG2E
    hq = nope + 2 * rope
    for h in range(MLA_HEADS):
        qh = _dot(qn, wq_ref[:, h * hq:(h + 1) * hq])
        q_ref[0, h] = (jnp.concatenate([qh[:, :nope], rotate(qh[:, nope:])], -1) * qscale).astype(BF16)
        kvh = _dot(kvn, wkv_ref[:, h * (nope + vd):(h + 1) * (nope + vd)])
        k_ref[0, h] = jnp.concatenate([kvh[:, :nope], k_rot], -1).astype(BF16)
        vt = kvh[:, nope:].T.astype(BF16)
        tk = v_ref.shape[-1]
        for c in range(v_ref.shape[2]):
            v_ref[0, h, c, 0:vd] = vt[:, c * tk:(c + 1) * tk]
            v_ref[0, h, c, vd:] = jnp.ones((v_ref.shape[3] - vd, tk), BF16)


def in_proj(x, w_in, w_q, w_kv, q_g, kv_g, rope_tab, *, batch, tm, lru_w):
    n, d = x.shape
    s = n // batch
    spb = s // tm
    q_rank, kv_rank = w_q.shape[0], w_kv.shape[0]
    qk = QK_NOPE_DIM + QK_ROPE_DIM
    tk = min(ATTN_TK, tm)
    const = lambda i: (0, 0)
    head_map = lambda i: (i // spb, 0, i % spb, 0)
    kern = functools.partial(_in_proj_kernel, lru_w=lru_w, q_rank=q_rank, kv_rank=kv_rank)
    return pl.pallas_call(
        kern,
        grid=(n // tm,),
        in_specs=[pl.BlockSpec((tm, d), lambda i: (i, 0)),
                  pl.BlockSpec(w_in.shape, const),
                  pl.BlockSpec(w_q.shape, const),
                  pl.BlockSpec(w_kv.shape, const),
                  pl.BlockSpec((1, q_rank), const),
                  pl.BlockSpec((1, kv_rank), const),
                  pl.BlockSpec((2 * QK_ROPE_DIM, tm), lambda i: (0, i))],
        out_specs=[pl.BlockSpec((tm, lru_w), lambda i: (i, 0)),
                   pl.BlockSpec((tm, lru_w), lambda i: (i, 0)),
                   pl.BlockSpec((1, MLA_HEADS, tm, qk), head_map),
                   pl.BlockSpec((1, MLA_HEADS, tm, qk), head_map),
                   pl.BlockSpec((1, MLA_HEADS, tm // tk, V_HEAD_DIM + ATTN_ONES_ROWS, tk),
                                lambda i: (i // spb, 0, i % spb, 0, 0))],
        out_shape=[jax.ShapeDtypeStruct((n, lru_w), F32),
                   jax.ShapeDtypeStruct((n, lru_w), BF16),
                   jax.ShapeDtypeStruct((batch, MLA_HEADS, s, qk), BF16),
                   jax.ShapeDtypeStruct((batch, MLA_HEADS, s, qk), BF16),
                   jax.ShapeDtypeStruct((batch, MLA_HEADS, s // tk, V_HEAD_DIM + ATTN_ONES_ROWS, tk), BF16)],
        compiler_params=_params("parallel"),
        name="in_proj",
    )(x, w_in, w_q, w_kv, q_g, kv_g, rope_tab)


def _rglru_kernel(x_ref, gate_ref, cw_ref, cb_ref, wa_ref, ba_ref, wx_ref, bx_ref, lam_ref,
                  y_ref, xbuf, a_scr, u_scr, h_scr, *, ts, pad):
    t = pl.program_id(1)
    w = x_ref.shape[-1]

    @pl.when(t == 0)
    def _():
        xbuf[0:pad, :] = jnp.zeros((pad, w), F32)
        h_scr[0:1, :] = jnp.zeros((1, w), F32)

    xbuf[pad:pad + ts, :] = x_ref[0]
    xc = cb_ref[...] + cw_ref[CONV_WIDTH - 1:CONV_WIDTH, :] * xbuf[pad:pad + ts, :]
    for k in range(CONV_WIDTH - 1):
        sh = CONV_WIDTH - 1 - k
        xc = xc + cw_ref[k:k + 1, :] * xbuf[pad - sh:pad - sh + ts, :]
    xbuf[0:pad, :] = xbuf[ts:ts + pad, :]

    xcb = xc.astype(BF16)
    ngrp = w // MXU_DIM
    rec = jnp.concatenate(
        [_dot(xcb[:, g * MXU_DIM:(g + 1) * MXU_DIM], wa_ref[g]) for g in range(ngrp)], -1)
    ing = jnp.concatenate(
        [_dot(xcb[:, g * MXU_DIM:(g + 1) * MXU_DIM], wx_ref[g]) for g in range(ngrp)], -1)
    rec = jax.nn.sigmoid(rec + ba_ref[...])
    ing = jax.nn.sigmoid(ing + bx_ref[...])

    nlam = -lam_ref[...]
    softplus = jnp.maximum(nlam, 0.0) + jnp.log1p(jnp.exp(-jnp.abs(nlam)))
    log_a = -LRU_C * rec * softplus
    a = jnp.exp(log_a)
    mult = jnp.sqrt(jnp.tanh(-log_a) * (1.0 + a * a))
    a_scr[...] = a
    u_scr[...] = mult * (ing * xc)

    def step(i, h):
        h = a_scr[pl.ds(i, 1), :] * h + u_scr[pl.ds(i, 1), :]
        u_scr[pl.ds(i, 1), :] = h
        return h

    h_last = lax.fori_loop(0, ts, step, h_scr[0:1, :], unroll=8)
    h_scr[0:1, :] = h_last
    y_ref[0] = (gate_ref[0].astype(F32) * u_scr[...]).astype(BF16)


def rglru(lru_in, gate, conv_w, conv_b, wa_bd, b_a, wx_bd, b_x, lam, *, ts):
    bsz, s, w = lru_in.shape
    pad = 8
    const2 = lambda b, t: (0, 0)
    const3 = lambda b, t: (0, 0, 0)
    kern = functools.partial(_rglru_kernel, ts=ts, pad=pad)
    return pl.pallas_call(
        kern,
        grid=(bsz, s // ts),
        in_specs=[pl.BlockSpec((1, ts, w), lambda b, t: (b, t, 0)),
                  pl.BlockSpec((1, ts, w), lambda b, t: (b, t, 0)),
                  pl.BlockSpec(conv_w.shape, const2),
                  pl.BlockSpec((1, w), const2),
                  pl.BlockSpec(wa_bd.shape, const3),
                  pl.BlockSpec((1, w), const2),
                  pl.BlockSpec(wx_bd.shape, const3),
                  pl.BlockSpec((1, w), const2),
                  pl.BlockSpec((1, w), const2)],
        out_specs=pl.BlockSpec((1, ts, w), lambda b, t: (b, t, 0)),
        out_shape=jax.ShapeDtypeStruct((bsz, s, w), BF16),
        scratch_shapes=[pltpu.VMEM((ts + pad, w), F32),
                        pltpu.VMEM((ts, w), F32),
                        pltpu.VMEM((ts, w), F32),
                        pltpu.VMEM((8, w), F32)],
        compiler_params=_params("parallel", "arbitrary"),
        name="rglru",
    )(lru_in, gate, conv_w, conv_b, wa_bd, b_a, wx_bd, b_x, lam)


def _block_diag_groups(wblk):
    g, d, _ = wblk.shape
    per = MXU_DIM // d
    wg = wblk.reshape(g // per, per, d, d)
    eye = jnp.eye(per, dtype=wblk.dtype)
    out = jnp.einsum('gpij,pq->gpiqj', wg, eye)
    return out.reshape(g // per, MXU_DIM, MXU_DIM)


HEADS_PER_STEP = 2
ATTN_TK = 256
ATTN_ONES_ROWS = 16
ATTN_KV_UNROLL = 2


def _attn_kernel(q_ref, k_ref, vt_ref, o_ref, *scratch, tq):
    nh, s, _ = q_ref.shape
    dva, tk = vt_ref.shape[2], vt_ref.shape[3]
    dv = dva - ATTN_ONES_ROWS
    nq, sub = s // tq, tq // tk
    assert sub % 2 == 0
    dims = (((1,), (1,)), ((), ()))
    heads = range(nh)
    st_scr = [scratch[2 * hh:2 * hh + 2] for hh in heads]
    acc_scr = scratch[2 * nh:]

    def scores(qs, kblk, slot):
        for hh in heads:
            st_scr[hh][slot][...] = lax.dot_general(k_ref[hh, pl.ds(kblk * tk, tk), :], qs[hh], dims,
                                                    preferred_element_type=F32)

    def update(hh, kblk, slot, diag_off, m):
        st = st_scr[hh][slot][...]
        if diag_off is not None:
            key = lax.broadcasted_iota(jnp.int32, (tk, tq), 0) + diag_off
            qry = lax.broadcasted_iota(jnp.int32, (tk, tq), 1)
            st = jnp.where(key <= qry, st, -jnp.inf)
        m_new = jnp.maximum(m, jnp.max(st, axis=0, keepdims=True))
        pt = jnp.exp2(st - m_new).astype(BF16)
        acc_scr[hh][...] = (jnp.exp2(m - m_new) * acc_scr[hh][...]
                            + _dot(vt_ref[hh, kblk], pt))
        return m_new

    def block(kblk, slot, diag_off, ms, ahead):
        scores(*ahead, 1 - slot)
        return tuple(update(hh, kblk, slot, diag_off, ms[hh]) for hh in heads)

    def load_q(qi):
        return [q_ref[hh, pl.ds(qi * tq, tq), :] for hh in heads]

    scores(load_q(0), 0, 0)

    def q_tile(qi, _):
        qs = load_q(qi)
        for hh in heads:
            acc_scr[hh][...] = jnp.zeros((dva, tq), F32)

        def kv_tiles(ntiles, kj, ms):
            for c in range(ntiles * sub):
                ms = block(kj * sub + c, c % 2, None, ms, (qs, kj * sub + c + 1))
            return ms

        n_main = qi // ATTN_KV_UNROLL
        ms = tuple(jnp.full((1, tq), -jnp.inf, F32) for _ in heads)
        ms = lax.fori_loop(0, n_main,
                           lambda it, ms: kv_tiles(ATTN_KV_UNROLL, it * ATTN_KV_UNROLL, ms), ms)
        ms = lax.fori_loop(n_main * ATTN_KV_UNROLL, qi, functools.partial(kv_tiles, 1), ms)
        q_next = load_q(jnp.minimum(qi + 1, nq - 1))
        for c in range(sub):
            ahead = (qs, qi * sub + c + 1) if c + 1 < sub else (q_next, 0)
            ms = block(qi * sub + c, c % 2, c * tk, ms, ahead)
        for hh in heads:
            out = acc_scr[hh][0:dv, :] / acc_scr[hh][dv:dv + 1, :]
            o_ref[pl.ds(qi * tq, tq), hh * dv:(hh + 1) * dv] = out.T.astype(o_ref.dtype)
        return 0

    lax.fori_loop(0, nq, q_tile, 0)


def attention(q, k, vt, *, tq):
    bsz, nh, s, dqk = q.shape
    _, _, nkb, dva, tk = vt.shape
    dv = dva - ATTN_ONES_ROWS
    hps = HEADS_PER_STEP
    return pl.pallas_call(
        functools.partial(_attn_kernel, tq=tq),
        grid=(bsz, nh // hps),
        in_specs=[pl.BlockSpec((None, hps, s, dqk), lambda b, h: (b, h, 0, 0)),
                  pl.BlockSpec((None, hps, s, dqk), lambda b, h: (b, h, 0, 0)),
                  pl.BlockSpec((None, hps, nkb, dva, tk), lambda b, h: (b, h, 0, 0, 0))],
        out_specs=pl.BlockSpec((None, s, hps * dv), lambda b, h: (b, 0, h)),
        out_shape=jax.ShapeDtypeStruct((bsz, s, nh * dv), BF16),
        scratch_shapes=([pltpu.VMEM((tk, tq), F32)] * (2 * hps) + [pltpu.VMEM((dva, tq), F32)] * hps),
        compiler_params=_params("parallel", "parallel"),
        name="mla_attention",
    )(q, k, vt)


def _row_chunks(nrows):
    step = min(MM_ROWS, nrows)
    return [slice(r, r + step) for r in range(0, nrows, step)]


def _out_proj_kernel(x_ref, yl_ref, ym_ref, wl_ref, wm_ref, g_ref, b_ref, o_ref):
    for r in _row_chunks(x_ref.shape[0]):
        mix = _dot(yl_ref[r, :], wl_ref[...]) + _dot(ym_ref[r, :], wm_ref[...])
        o_ref[r, :] = _layer_norm(DEEPNORM_ALPHA * x_ref[r, :] + mix, g_ref[...], b_ref[...])


def out_proj_ln(x, y_lru, y_mla, w_lru, w_mla, g, b, *, tm):
    n, d = x.shape
    wl, wm = y_lru.shape[1], y_mla.shape[1]
    const = lambda i: (0, 0)
    return pl.pallas_call(
        _out_proj_kernel,
        grid=(n // tm,),
        in_specs=[pl.BlockSpec((tm, d), lambda i: (i, 0)),
                  pl.BlockSpec((tm, wl), lambda i: (i, 0)),
                  pl.BlockSpec((tm, wm), lambda i: (i, 0)),
                  pl.BlockSpec((wl, d), const),
                  pl.BlockSpec((wm, d), const),
                  pl.BlockSpec((1, d), const),
                  pl.BlockSpec((1, d), const)],
        out_specs=pl.BlockSpec((tm, d), lambda i: (i, 0)),
        out_shape=jax.ShapeDtypeStruct((n, d), F32),
        compiler_params=_params("parallel"),
        name="out_proj_ln",
    )(x, y_lru, y_mla, w_lru, w_mla, g, b)


def _ple_kernel(x_ref, p_ref, wg_ref, bg_ref, wp_ref, g_ref, b_ref, o_ref):
    for r in _row_chunks(x_ref.shape[0]):
        x = x_ref[r, :]
        gate = jax.nn.sigmoid(_dot(x.astype(BF16), wg_ref[...]) + bg_ref[...])
        proj = _dot(p_ref[r, :].astype(BF16), wp_ref[...])
        o_ref[r, :] = _layer_norm(DEEPNORM_ALPHA * x + gate * proj, g_ref[...], b_ref[...])


def ple_ln(x, p, w_gate, b_gate, w_proj, g, b, *, tm):
    n, d = x.shape
    pd = p.shape[1]
    const = lambda i: (0, 0)
    return pl.pallas_call(
        _ple_kernel,
        grid=(n // tm,),
        in_specs=[pl.BlockSpec((tm, d), lambda i: (i, 0)),
                  pl.BlockSpec((tm, pd), lambda i: (i, 0)),
                  pl.BlockSpec((d, d), const),
                  pl.BlockSpec((1, d), const),
                  pl.BlockSpec((pd, d), const),
                  pl.BlockSpec((1, d), const),
                  pl.BlockSpec((1, d), const)],
        out_specs=pl.BlockSpec((tm, d), lambda i: (i, 0)),
        out_shape=jax.ShapeDtypeStruct((n, d), F32),
        compiler_params=_params("parallel"),
        name="ple_ln",
    )(x, p, w_gate, b_gate, w_proj, g, b)


def _swap_halves_cols(w):
    half = w.shape[-1] // 2
    return jnp.concatenate([w[..., half:], w[..., :half]], -1)


def _tile(n, pref):
    t = min(n, pref)
    assert n % t == 0
    return t


def _layer(x, p, rope_tab, prm, *, batch):
    n, d = x.shape
    row = lambda v: v.reshape(1, -1)
    tm = _tile(n // batch, 512)
    tm_ffn = _tile(n, 1024)

    x = ffn_ln(x, prm["ffn1_w_gate"].astype(BF16), prm["ffn1_w_up"].astype(BF16),
               prm["ffn1_w_down"].astype(BF16), row(prm["ln1_g"]), row(prm["ln1_b"]),
               tm=tm_ffn, tf=_tile(prm["ffn1_w_gate"].shape[1], 512))

    lru_w = prm["conv_w"].shape[1]
    w_in = prm["w_in"]
    w_in = w_in.astype(BF16)
    w_in_ext = jnp.concatenate([w_in, _swap_halves_cols(w_in[:, -QK_ROPE_DIM:])], -1)
    q_rank = prm["w_q_up"].shape[0]
    wq = prm["w_q_up"].astype(BF16).reshape(q_rank, MLA_HEADS, QK_NOPE_DIM + QK_ROPE_DIM)
    wq_ext = jnp.concatenate([wq, _swap_halves_cols(wq[..., QK_NOPE_DIM:])], -1)
    wq_ext = wq_ext.reshape(q_rank, -1)
    lru_in, gate, q, k, vt = in_proj(
        x, w_in_ext, wq_ext, prm["w_kv_up"].astype(BF16), row(prm["q_norm_g"]),
        row(prm["kv_norm_g"]), rope_tab, batch=batch, tm=tm, lru_w=lru_w)

    s = n // batch
    y_lru = rglru(lru_in.reshape(batch, s, lru_w), gate.reshape(batch, s, lru_w),
                  prm["conv_w"], row(prm["conv_b"]),
                  _block_diag_groups(prm["lru_w_a"]).astype(BF16), row(prm["lru_b_a"]),
                  _block_diag_groups(prm["lru_w_x"]).astype(BF16), row(prm["lru_b_x"]),
                  row(prm["lru_lambda"]), ts=_tile(s, 512))
    y_mla = attention(q, k, vt, tq=tm)

    w_out = prm["w_out"].astype(BF16)
    x = out_proj_ln(x, y_lru.reshape(n, lru_w), y_mla.reshape(n, -1), w_out[:lru_w], w_out[lru_w:],
                    row(prm["ln2_g"]), row(prm["ln2_b"]), tm=tm)

    x = ffn_ln(x, prm["ffn2_w_gate"].astype(BF16), prm["ffn2_w_up"].astype(BF16),
               prm["ffn2_w_down"].astype(BF16), row(prm["ln3_g"]), row(prm["ln3_b"]),
               tm=tm_ffn, tf=_tile(prm["ffn2_w_gate"].shape[1], 512))

    return ple_ln(x, p, prm["ple_w_gate"].astype(BF16), row(prm["ple_b_gate"]),
                  prm["ple_w_proj"].astype(BF16), row(prm["ln4_g"]), row(prm["ln4_b"]), tm=tm)


def kernel(x, p, positions, ffn1_w_gate, ffn1_w_up, ffn1_w_down, ln1_g, ln1_b, w_in, conv_w, conv_b, lru_w_a, lru_b_a, lru_w_x, lru_b_x, lru_lambda, q_norm_g, w_q_up, kv_norm_g, w_kv_up, w_out, ln2_g, ln2_b, ffn2_w_gate, ffn2_w_up, ffn2_w_down, ln3_g, ln3_b, ple_w_gate, ple_b_gate, ple_w_proj, ln4_g, ln4_b):
    names = ("ffn1_w_gate ffn1_w_up ffn1_w_down ln1_g ln1_b w_in conv_w conv_b lru_w_a lru_b_a lru_w_x "
             "lru_b_x lru_lambda q_norm_g w_q_up kv_norm_g w_kv_up w_out ln2_g ln2_b ffn2_w_gate ffn2_w_up "
             "ffn2_w_down ln3_g ln3_b ple_w_gate ple_b_gate ple_w_proj ln4_g ln4_b").split()
    stacked = dict(zip(names, (ffn1_w_gate, ffn1_w_up, ffn1_w_down, ln1_g, ln1_b, w_in, conv_w, conv_b,
                               lru_w_a, lru_b_a, lru_w_x, lru_b_x, lru_lambda, q_norm_g, w_q_up, kv_norm_g,
                               w_kv_up, w_out, ln2_g, ln2_b, ffn2_w_gate, ffn2_w_up, ffn2_w_down, ln3_g,
                               ln3_b, ple_w_gate, ple_b_gate, ple_w_proj, ln4_g, ln4_b)))
    batch, seq, d = x.shape
    rope_tab = rope_table(positions, tn=_tile(batch * seq, 4096))
    h = x.reshape(batch * seq, d)
    assert ffn1_w_gate.shape[0] == DEPTH
    for i in range(DEPTH):
        prm = {kname: val[i] for kname, val in stacked.items()}
        h = _layer(h, p[i].reshape(batch * seq, -1), rope_tab, prm, batch=batch)
    return h.reshape(batch, seq, d)
```

```python
import functools
import math

import jax
import jax.numpy as jnp
import numpy as np
from jax import lax
from jax.experimental import pallas as pl
from jax.experimental.pallas import tpu as pltpu

LRU_BLOCKS = 16
CONV_WIDTH = 4
LRU_C = 8.0
MLA_HEADS = 8
QK_NOPE_DIM = 128
QK_ROPE_DIM = 64
V_HEAD_DIM = 128
ROPE_THETA = 10000.0
LN_EPS = 1e-5
RMS_EPS = 1e-6
DEPTH = 1
DEEPNORM_ALPHA = (2 * DEPTH) ** 0.25

LANES = 128
MXU_DIM = 256
VMEM_LIMIT = 61 * 1024 * 1024
LOG2E = math.log2(math.e)
LN_ROWS = 128
MM_ROWS = 256

BF16 = jnp.bfloat16
F32 = jnp.float32


def _params(*sem):
    return pltpu.CompilerParams(dimension_semantics=sem, vmem_limit_bytes=VMEM_LIMIT)


def _layer_norm(y, g, b):
    mu = jnp.mean(y, axis=-1, keepdims=True)
    yc = y - mu
    var = jnp.mean(yc * yc, axis=-1, keepdims=True)
    return yc * lax.rsqrt(var + LN_EPS) * g + b


def _rms_norm(y, g):
    return y * lax.rsqrt(jnp.mean(y * y, axis=-1, keepdims=True) + RMS_EPS) * g


def _dot(a, b):
    return jnp.dot(a, b, preferred_element_type=F32)


def _sigmoid(x):
    return 0.5 * jnp.tanh(0.5 * x) + 0.5


def _rope_kernel(inv_ref, pos_ref, tab_ref):
    half = inv_ref.shape[0]
    ang = pos_ref[...].astype(F32) * inv_ref[...]
    cos = jnp.cos(ang)
    sin = jnp.sin(ang)
    tab_ref[0 * half:1 * half, :] = cos
    tab_ref[1 * half:2 * half, :] = cos
    tab_ref[2 * half:3 * half, :] = -sin
    tab_ref[3 * half:4 * half, :] = sin


def rope_table(positions, *, tn):
    n = positions.size
    half = QK_ROPE_DIM // 2
    inv_freq = ROPE_THETA ** (-jnp.arange(0, QK_ROPE_DIM, 2, dtype=F32) / QK_ROPE_DIM)
    return pl.pallas_call(
        _rope_kernel,
        grid=(n // tn,),
        in_specs=[pl.BlockSpec((half, 1), lambda i: (0, 0)),
                  pl.BlockSpec((1, tn), lambda i: (0, i))],
        out_specs=pl.BlockSpec((4 * half, tn), lambda i: (0, i)),
        out_shape=jax.ShapeDtypeStruct((4 * half, n), F32),
        compiler_params=_params("parallel"),
        name="rope_table",
    )(inv_freq.reshape(half, 1), positions.reshape(1, n))


def _ffn_ln_kernel(x_ref, wg_ref, wu_ref, wd_ref, g_ref, b_ref, o_ref, xb_ref):
    j = pl.program_id(1)

    @pl.when(j == 0)
    def _():
        xb_ref[...] = x_ref[...].astype(BF16)
        o_ref[...] = jnp.zeros_like(o_ref)

    xb = xb_ref[...]
    gate = _dot(xb, wg_ref[...])
    up = _dot(xb, wu_ref[...])
    h = (gate * jax.nn.sigmoid(gate) * up).astype(BF16)
    o_ref[...] += _dot(h, wd_ref[...])

    @pl.when(j == pl.num_programs(1) - 1)
    def _():
        def rows(c, _):
            r = pl.ds(pl.multiple_of(c * LN_ROWS, LN_ROWS), LN_ROWS)
            y = DEEPNORM_ALPHA * x_ref[r, :] + 0.5 * o_ref[r, :]
            o_ref[r, :] = _layer_norm(y, g_ref[...], b_ref[...])
            return 0

        lax.fori_loop(0, o_ref.shape[0] // LN_ROWS, rows, 0)


def ffn_ln(x, wg, wu, wd, g, b, *, tm, tf):
    n, d = x.shape
    f = wg.shape[1]
    return pl.pallas_call(
        _ffn_ln_kernel,
        grid=(n // tm, f // tf),
        in_specs=[pl.BlockSpec((tm, d), lambda i, j: (i, 0)),
                  pl.BlockSpec((d, tf), lambda i, j: (0, j)),
                  pl.BlockSpec((d, tf), lambda i, j: (0, j)),
                  pl.BlockSpec((tf, d), lambda i, j: (j, 0)),
                  pl.BlockSpec((1, d), lambda i, j: (0, 0)),
                  pl.BlockSpec((1, d), lambda i, j: (0, 0))],
        out_specs=pl.BlockSpec((tm, d), lambda i, j: (i, 0)),
        out_shape=jax.ShapeDtypeStruct((n, d), F32),
        scratch_shapes=[pltpu.VMEM((tm, d), BF16)],
        compiler_params=_params("parallel", "arbitrary"),
        name="ffn_ln",
    )(x, wg, wu, wd, g, b)


def _in_proj_kernel(x_ref, win_ref, wq_ref, wkv_ref, qg_ref, kvg_ref, rope_ref,
                    lru_ref, gate_ref, q_ref, k_ref, v_ref, *, lru_w, q_rank, kv_rank):
    xb = x_ref[...].astype(BF16)
    nope, rope, vd = QK_NOPE_DIM, QK_ROPE_DIM, V_HEAD_DIM
    tab = rope_ref[...].T

    def rotate(pe_and_swapped):
        prod = pe_and_swapped * tab
        return prod[:, :rope] + prod[:, rope:]

    lru_ref[...] = _dot(xb, win_ref[:, 0:lru_w])
    gate_ref[...] = jax.nn.gelu(_dot(xb, win_ref[:, lru_w:2 * lru_w])).astype(BF16)

    o = 2 * lru_w
    c_q = _dot(xb, win_ref[:, o:o + q_rank])
    o += q_rank
    c_kv = _dot(xb, win_ref[:, o:o + kv_rank])
    o += kv_rank
    k_rot = rotate(_dot(xb, win_ref[:, o:o + 2 * rope]))

    qn = _rms_norm(c_q, qg_ref[...]).astype(BF16)
    kvn = _rms_norm(c_kv, kvg_ref[...]).astype(BF16)
    qscale = (nope + rope) ** -0.5 * LOG2E
    hq = nope + 2 * rope
    for h in range(MLA_HEADS):
        qh = _dot(qn, wq_ref[:, h * hq:(h + 1) * hq])
        qrot = jnp.concatenate([qh[:, :nope], rotate(qh[:, nope:])], -1) * qscale
        q_ref[0, h, 0] = qrot.T.astype(BF16)
        kvh = _dot(kvn, wkv_ref[:, h * (nope + vd):(h + 1) * (nope + vd)])
        k_ref[0, h] = jnp.concatenate([kvh[:, :nope], k_rot], -1).astype(BF16)
        vt = kvh[:, nope:].T.astype(BF16)
        tk = v_ref.shape[-1]
        for c in range(v_ref.shape[2]):
            v_ref[0, h, c, 0:vd] = vt[:, c * tk:(c + 1) * tk]
            v_ref[0, h, c, vd:] = jnp.ones((v_ref.shape[3] - vd, tk), BF16)


def in_proj(x, w_in, w_q, w_kv, q_g, kv_g, rope_tab, *, batch, tm, lru_w):
    n, d = x.shape
    s = n // batch
    spb = s // tm
    q_rank, kv_rank = w_q.shape[0], w_kv.shape[0]
    qk = QK_NOPE_DIM + QK_ROPE_DIM
    tk = min(ATTN_TK, tm)
    const = lambda i: (0, 0)
    head_map = lambda i: (i // spb, 0, i % spb, 0)
    kern = functools.partial(_in_proj_kernel, lru_w=lru_w, q_rank=q_rank, kv_rank=kv_rank)
    return pl.pallas_call(
        kern,
        grid=(n // tm,),
        in_specs=[pl.BlockSpec((tm, d), lambda i: (i, 0)),
                  pl.BlockSpec(w_in.shape, const),
                  pl.BlockSpec(w_q.shape, const),
                  pl.BlockSpec(w_kv.shape, const),
                  pl.BlockSpec((1, q_rank), const),
                  pl.BlockSpec((1, kv_rank), const),
                  pl.BlockSpec((2 * QK_ROPE_DIM, tm), lambda i: (0, i))],
        out_specs=[pl.BlockSpec((tm, lru_w), lambda i: (i, 0)),
                   pl.BlockSpec((tm, lru_w), lambda i: (i, 0)),
                   pl.BlockSpec((1, MLA_HEADS, 1, qk, tm), lambda i: (i // spb, 0, i % spb, 0, 0)),
                   pl.BlockSpec((1, MLA_HEADS, tm, qk), head_map),
                   pl.BlockSpec((1, MLA_HEADS, tm // tk, V_HEAD_DIM + ATTN_ONES_ROWS, tk),
                                lambda i: (i // spb, 0, i % spb, 0, 0))],
        out_shape=[jax.ShapeDtypeStruct((n, lru_w), F32),
                   jax.ShapeDtypeStruct((n, lru_w), BF16),
                   jax.ShapeDtypeStruct((batch, MLA_HEADS, spb, qk, tm), BF16),
                   jax.ShapeDtypeStruct((batch, MLA_HEADS, s, qk), BF16),
                   jax.ShapeDtypeStruct((batch, MLA_HEADS, s // tk, V_HEAD_DIM + ATTN_ONES_ROWS, tk), BF16)],
        compiler_params=_params("parallel"),
        name="in_proj",
    )(x, w_in, w_q, w_kv, q_g, kv_g, rope_tab)


def _rglru_kernel(x_ref, gate_ref, cw_ref, cb_ref, wa_ref, ba_ref, wx_ref, bx_ref, lam_ref,
                  y_ref, xbuf, a_scr, u_scr, h_scr, *, ts, pad):
    t = pl.program_id(1)
    w = x_ref.shape[-1]

    @pl.when(t == 0)
    def _():
        xbuf[0:pad, :] = jnp.zeros((pad, w), F32)
        h_scr[0:1, :] = jnp.zeros((1, w), F32)

    xbuf[pad:pad + ts, :] = x_ref[0]
    xc = cb_ref[...] + cw_ref[CONV_WIDTH - 1:CONV_WIDTH, :] * xbuf[pad:pad + ts, :]
    for k in range(CONV_WIDTH - 1):
        sh = CONV_WIDTH - 1 - k
        xc = xc + cw_ref[k:k + 1, :] * xbuf[pad - sh:pad - sh + ts, :]
    xbuf[0:pad, :] = xbuf[ts:ts + pad, :]

    xcb = xc.astype(BF16)
    ngrp = w // MXU_DIM
    rec = jnp.concatenate(
        [_dot(xcb[:, g * MXU_DIM:(g + 1) * MXU_DIM], wa_ref[g]) for g in range(ngrp)], -1)
    ing = jnp.concatenate(
        [_dot(xcb[:, g * MXU_DIM:(g + 1) * MXU_DIM], wx_ref[g]) for g in range(ngrp)], -1)
    rec = _sigmoid(rec + ba_ref[...])
    ing = _sigmoid(ing + bx_ref[...])

    nlam = -lam_ref[...]
    softplus = jnp.maximum(nlam, 0.0) + jnp.log1p(jnp.exp(-jnp.abs(nlam)))
    log_a = -LRU_C * rec * softplus
    a = jnp.exp(log_a)
    mult = jnp.sqrt(jnp.tanh(-log_a) * (1.0 + a * a))
    a_scr[...] = a
    u_scr[...] = mult * (ing * xc)

    def step(i, h):
        r0, r1 = pl.ds(2 * i, 1), pl.ds(2 * i + 1, 1)
        a0, u0, a1, u1 = a_scr[r0, :], u_scr[r0, :], a_scr[r1, :], u_scr[r1, :]
        u_scr[r0, :] = a0 * h + u0
        h = (a1 * a0) * h + (a1 * u0 + u1)
        u_scr[r1, :] = h
        return h

    h_last = lax.fori_loop(0, ts // 2, step, h_scr[0:1, :], unroll=16)
    h_scr[0:1, :] = h_last
    y_ref[0] = (gate_ref[0].astype(F32) * u_scr[...]).astype(BF16)


def rglru(lru_in, gate, conv_w, conv_b, wa_bd, b_a, wx_bd, b_x, lam, *, ts):
    bsz, s, w = lru_in.shape
    pad = 8
    const2 = lambda b, t: (0, 0)
    const3 = lambda b, t: (0, 0, 0)
    kern = functools.partial(_rglru_kernel, ts=ts, pad=pad)
    return pl.pallas_call(
        kern,
        grid=(bsz, s // ts),
        in_specs=[pl.BlockSpec((1, ts, w), lambda b, t: (b, t, 0)),
                  pl.BlockSpec((1, ts, w), lambda b, t: (b, t, 0)),
                  pl.BlockSpec(conv_w.shape, const2),
                  pl.BlockSpec((1, w), const2),
                  pl.BlockSpec(wa_bd.shape, const3),
                  pl.BlockSpec((1, w), const2),
                  pl.BlockSpec(wx_bd.shape, const3),
                  pl.BlockSpec((1, w), const2),
                  pl.BlockSpec((1, w), const2)],
        out_specs=pl.BlockSpec((1, ts, w), lambda b, t: (b, t, 0)),
        out_shape=jax.ShapeDtypeStruct((bsz, s, w), BF16),
        scratch_shapes=[pltpu.VMEM((ts + pad, w), F32),
                        pltpu.VMEM((ts, w), F32),
                        pltpu.VMEM((ts, w), F32),
                        pltpu.VMEM((8, w), F32)],
        compiler_params=_params("parallel", "arbitrary"),
        name="rglru",
    )(lru_in, gate, conv_w, conv_b, wa_bd, b_a, wx_bd, b_x, lam)


def _block_diag_groups(wblk):
    g, d, _ = wblk.shape
    per = MXU_DIM // d
    wg = wblk.reshape(g // per, per, d, d)
    eye = jnp.eye(per, dtype=wblk.dtype)
    out = jnp.einsum('gpij,pq->gpiqj', wg, eye)
    return out.reshape(g // per, MXU_DIM, MXU_DIM)


HEADS_PER_STEP = 2
ATTN_TK = 256
ATTN_ONES_ROWS = 16
ATTN_KV_UNROLL = 2


def _attn_kernel(q_ref, k_ref, vt_ref, o_ref, *scratch, tq):
    nh, nq = q_ref.shape[0], q_ref.shape[1]
    dva, tk = vt_ref.shape[2], vt_ref.shape[3]
    dv = dva - ATTN_ONES_ROWS
    sub = tq // tk
    assert sub % 2 == 0
    heads = range(nh)
    st_scr = [scratch[2 * hh:2 * hh + 2] for hh in heads]
    acc_scr = scratch[2 * nh:]

    def scores(qs, kblk, slot):
        for hh in heads:
            st_scr[hh][slot][...] = _dot(k_ref[hh, pl.ds(kblk * tk, tk), :], qs[hh])

    def update(hh, kblk, slot, diag_off, m):
        st = st_scr[hh][slot][...]
        if diag_off is not None:
            key = lax.broadcasted_iota(jnp.int32, (tk, tq), 0) + diag_off
            qry = lax.broadcasted_iota(jnp.int32, (tk, tq), 1)
            st = jnp.where(key <= qry, st, -jnp.inf)
        m_new = jnp.maximum(m, jnp.max(st, axis=0, keepdims=True))
        pt = jnp.exp2(st - m_new).astype(BF16)
        acc_scr[hh][...] = (jnp.exp2(m - m_new) * acc_scr[hh][...]
                            + _dot(vt_ref[hh, kblk], pt))
        return m_new

    def block(kblk, slot, diag_off, ms, ahead):
        scores(*ahead, 1 - slot)
        return tuple(update(hh, kblk, slot, diag_off, ms[hh]) for hh in heads)

    def load_q(qi):
        return [q_ref[hh, qi] for hh in heads]

    scores(load_q(0), 0, 0)

    def q_tile(qi, _):
        qs = load_q(qi)
        for hh in heads:
            acc_scr[hh][...] = jnp.zeros((dva, tq), F32)

        def kv_tiles(ntiles, kj, ms):
            for c in range(ntiles * sub):
                ms = block(kj * sub + c, c % 2, None, ms, (qs, kj * sub + c + 1))
            return ms

        n_main = qi // ATTN_KV_UNROLL
        ms = tuple(jnp.full((1, tq), -jnp.inf, F32) for _ in heads)
        ms = lax.fori_loop(0, n_main,
                           lambda it, ms: kv_tiles(ATTN_KV_UNROLL, it * ATTN_KV_UNROLL, ms), ms)
        ms = lax.fori_loop(n_main * ATTN_KV_UNROLL, qi, functools.partial(kv_tiles, 1), ms)
        q_next = load_q(jnp.minimum(qi + 1, nq - 1))
        for c in range(sub):
            ahead = (qs, qi * sub + c + 1) if c + 1 < sub else (q_next, 0)
            ms = block(qi * sub + c, c % 2, c * tk, ms, ahead)
        for hh in heads:
            out = acc_scr[hh][0:dv, :] / acc_scr[hh][dv:dv + 1, :]
            o_ref[pl.ds(qi * tq, tq), hh * dv:(hh + 1) * dv] = out.T.astype(o_ref.dtype)
        return 0

    lax.fori_loop(0, nq, q_tile, 0)


def attention(qt, k, vt):
    bsz, nh, nq, dqk, tq = qt.shape
    s = nq * tq
    _, _, nkb, dva, tk = vt.shape
    dv = dva - ATTN_ONES_ROWS
    hps = HEADS_PER_STEP
    return pl.pallas_call(
        functools.partial(_attn_kernel, tq=tq),
        grid=(bsz, nh // hps),
        in_specs=[pl.BlockSpec((None, hps, nq, dqk, tq), lambda b, h: (b, h, 0, 0, 0)),
                  pl.BlockSpec((None, hps, s, dqk), lambda b, h: (b, h, 0, 0)),
                  pl.BlockSpec((None, hps, nkb, dva, tk), lambda b, h: (b, h, 0, 0, 0))],
        out_specs=pl.BlockSpec((None, s, hps * dv), lambda b, h: (b, 0, h)),
        out_shape=jax.ShapeDtypeStruct((bsz, s, nh * dv), BF16),
        scratch_shapes=([pltpu.VMEM((tk, tq), F32)] * (2 * hps) + [pltpu.VMEM((dva, tq), F32)] * hps),
        compiler_params=_params("parallel", "parallel"),
        name="mla_attention",
    )(qt, k, vt)


def _row_chunks(nrows):
    step = min(MM_ROWS, nrows)
    return [slice(r, r + step) for r in range(0, nrows, step)]


def _out_proj_kernel(x_ref, yl_ref, ym_ref, wl_ref, wm_ref, g_ref, b_ref, o_ref):
    for r in _row_chunks(x_ref.shape[0]):
        mix = _dot(yl_ref[r, :], wl_ref[...]) + _dot(ym_ref[r, :], wm_ref[...])
        o_ref[r, :] = _layer_norm(DEEPNORM_ALPHA * x_ref[r, :] + mix, g_ref[...], b_ref[...])


def out_proj_ln(x, y_lru, y_mla, w_lru, w_mla, g, b, *, tm):
    n, d = x.shape
    wl, wm = y_lru.shape[1], y_mla.shape[1]
    const = lambda i: (0, 0)
    return pl.pallas_call(
        _out_proj_kernel,
        grid=(n // tm,),
        in_specs=[pl.BlockSpec((tm, d), lambda i: (i, 0)),
                  pl.BlockSpec((tm, wl), lambda i: (i, 0)),
                  pl.BlockSpec((tm, wm), lambda i: (i, 0)),
                  pl.BlockSpec((wl, d), const),
                  pl.BlockSpec((wm, d), const),
                  pl.BlockSpec((1, d), const),
                  pl.BlockSpec((1, d), const)],
        out_specs=pl.BlockSpec((tm, d), lambda i: (i, 0)),
        out_shape=jax.ShapeDtypeStruct((n, d), F32),
        compiler_params=_params("parallel"),
        name="out_proj_ln",
    )(x, y_lru, y_mla, w_lru, w_mla, g, b)


def _ple_kernel(x_ref, p_ref, wg_ref, bg_ref, wp_ref, g_ref, b_ref, o_ref):
    for r in _row_chunks(x_ref.shape[0]):
        x = x_ref[r, :]
        gate = _sigmoid(_dot(x.astype(BF16), wg_ref[...]) + bg_ref[...])
        proj = _dot(p_ref[r, :].astype(BF16), wp_ref[...])
        o_ref[r, :] = _layer_norm(DEEPNORM_ALPHA * x + gate * proj, g_ref[...], b_ref[...])


def ple_ln(x, p, w_gate, b_gate, w_proj, g, b, *, tm):
    n, d = x.shape
    pd = p.shape[1]
    const = lambda i: (0, 0)
    return pl.pallas_call(
        _ple_kernel,
        grid=(n // tm,),
        in_specs=[pl.BlockSpec((tm, d), lambda i: (i, 0)),
                  pl.BlockSpec((tm, pd), lambda i: (i, 0)),
                  pl.BlockSpec((d, d), const),
                  pl.BlockSpec((1, d), const),
                  pl.BlockSpec((pd, d), const),
                  pl.BlockSpec((1, d), const),
                  pl.BlockSpec((1, d), const)],
        out_specs=pl.BlockSpec((tm, d), lambda i: (i, 0)),
        out_shape=jax.ShapeDtypeStruct((n, d), F32),
        compiler_params=_params("parallel"),
        name="ple_ln",
    )(x, p, w_gate, b_gate, w_proj, g, b)


def _swap_halves_cols(w):
    half = w.shape[-1] // 2
    return jnp.concatenate([w[..., half:], w[..., :half]], -1)


def _tile(n, pref):
    t = min(n, pref)
    assert n % t == 0
    return t


def _layer(x, p, rope_tab, prm, *, batch):
    n, d = x.shape
    row = lambda v: v.reshape(1, -1)
    tm = _tile(n // batch, 512)
    tm_ffn = _tile(n, 1024)

    x = ffn_ln(x, prm["ffn1_w_gate"].astype(BF16), prm["ffn1_w_up"].astype(BF16),
               prm["ffn1_w_down"].astype(BF16), row(prm["ln1_g"]), row(prm["ln1_b"]),
               tm=tm_ffn, tf=_tile(prm["ffn1_w_gate"].shape[1], 512))

    lru_w = prm["conv_w"].shape[1]
    w_in = prm["w_in"]
    w_in = w_in.astype(BF16)
    w_in_ext = jnp.concatenate([w_in, _swap_halves_cols(w_in[:, -QK_ROPE_DIM:])], -1)
    q_rank = prm["w_q_up"].shape[0]
    wq = prm["w_q_up"].astype(BF16).reshape(q_rank, MLA_HEADS, QK_NOPE_DIM + QK_ROPE_DIM)
    wq_ext = jnp.concatenate([wq, _swap_halves_cols(wq[..., QK_NOPE_DIM:])], -1)
    wq_ext = wq_ext.reshape(q_rank, -1)
    lru_in, gate, q, k, vt = in_proj(
        x, w_in_ext, wq_ext, prm["w_kv_up"].astype(BF16), row(prm["q_norm_g"]),
        row(prm["kv_norm_g"]), rope_tab, batch=batch, tm=tm, lru_w=lru_w)

    s = n // batch
    y_lru = rglru(lru_in.reshape(batch, s, lru_w), gate.reshape(batch, s, lru_w),
                  prm["conv_w"], row(prm["conv_b"]),
                  _block_diag_groups(prm["lru_w_a"]).astype(BF16), row(prm["lru_b_a"]),
                  _block_diag_groups(prm["lru_w_x"]).astype(BF16), row(prm["lru_b_x"]),
                  row(prm["lru_lambda"]), ts=_tile(s, 512))
    y_mla = attention(q, k, vt)

    w_out = prm["w_out"].astype(BF16)
    x = out_proj_ln(x, y_lru.reshape(n, lru_w), y_mla.reshape(n, -1), w_out[:lru_w], w_out[lru_w:],
                    row(prm["ln2_g"]), row(prm["ln2_b"]), tm=tm)

    x = ffn_ln(x, prm["ffn2_w_gate"].astype(BF16), prm["ffn2_w_up"].astype(BF16),
               prm["ffn2_w_down"].astype(BF16), row(prm["ln3_g"]), row(prm["ln3_b"]),
               tm=tm_ffn, tf=_tile(prm["ffn2_w_gate"].shape[1], 512))

    return ple_ln(x, p, prm["ple_w_gate"].astype(BF16), row(prm["ple_b_gate"]),
                  prm["ple_w_proj"].astype(BF16), row(prm["ln4_g"]), row(prm["ln4_b"]), tm=tm)


def kernel(x, p, positions, ffn1_w_gate, ffn1_w_up, ffn1_w_down, ln1_g, ln1_b, w_in, conv_w, conv_b, lru_w_a, lru_b_a, lru_w_x, lru_b_x, lru_lambda, q_norm_g, w_q_up, kv_norm_g, w_kv_up, w_out, ln2_g, ln2_b, ffn2_w_gate, ffn2_w_up, ffn2_w_down, ln3_g, ln3_b, ple_w_gate, ple_b_gate, ple_w_proj, ln4_g, ln4_b):
    names = ("ffn1_w_gate ffn1_w_up ffn1_w_down ln1_g ln1_b w_in conv_w conv_b lru_w_a lru_b_a lru_w_x "
             "lru_b_x lru_lambda q_norm_g w_q_up kv_norm_g w_kv_up w_out ln2_g ln2_b ffn2_w_gate ffn2_w_up "
             "ffn2_w_down ln3_g ln3_b ple_w_gate ple_b_gate ple_w_proj ln4_g ln4_b").split()
    stacked = dict(zip(names, (ffn1_w_gate, ffn1_w_up, ffn1_w_down, ln1_g, ln1_b, w_in, conv_w, conv_b,
                               lru_w_a, lru_b_a, lru_w_x, lru_b_x, lru_lambda, q_norm_g, w_q_up, kv_norm_g,
                               w_kv_up, w_out, ln2_g, ln2_b, ffn2_w_gate, ffn2_w_up, ffn2_w_down, ln3_g,
                               ln3_b, ple_w_gate, ple_b_gate, ple_w_proj, ln4_g, ln4_b)))
    batch, seq, d = x.shape
    rope_tab = rope_table(positions, tn=_tile(batch * seq, 4096))
    h = x.reshape(batch * seq, d)
    assert ffn1_w_gate.shape[0] == DEPTH
    for i in range(DEPTH):
        prm = {kname: val[i] for kname, val in stacked.items()}
        h = _layer(h, p[i].reshape(batch * seq, -1), rope_tab, prm, batch=batch)
    return h.reshape(batch, seq, d)
```

```python
import functools
import math

import jax
import jax.numpy as jnp
import numpy as np
from jax import lax
from jax.experimental import pallas as pl
from jax.experimental.pallas import tpu as pltpu

LRU_BLOCKS = 16
CONV_WIDTH = 4
LRU_C = 8.0
MLA_HEADS = 8
QK_NOPE_DIM = 128
QK_ROPE_DIM = 64
V_HEAD_DIM = 128
ROPE_THETA = 10000.0
LN_EPS = 1e-5
RMS_EPS = 1e-6
DEPTH = 1
DEEPNORM_ALPHA = (2 * DEPTH) ** 0.25

LANES = 128
MXU_DIM = 256
VMEM_LIMIT = 61 * 1024 * 1024
LOG2E = math.log2(math.e)
LN_ROWS = 128
MM_ROWS = 256
FFN_CAST_SCALES = (1.0, 1.0, 0.5)

BF16 = jnp.bfloat16
F32 = jnp.float32


def _params(*sem):
    return pltpu.CompilerParams(dimension_semantics=sem, vmem_limit_bytes=VMEM_LIMIT)


def _layer_norm(y, g, b):
    mu = jnp.mean(y, axis=-1, keepdims=True)
    yc = y - mu
    var = jnp.mean(yc * yc, axis=-1, keepdims=True)
    return yc * lax.rsqrt(var + LN_EPS) * g + b


def _rms_norm(y, g):
    return y * lax.rsqrt(jnp.mean(y * y, axis=-1, keepdims=True) + RMS_EPS) * g


def _dot(a, b):
    return jnp.dot(a, b, preferred_element_type=F32)


def _sigmoid(x):
    return 0.5 * jnp.tanh(0.5 * x) + 0.5


def _rope_kernel(inv_ref, pos_ref, tab_ref):
    half = inv_ref.shape[0]
    ang = pos_ref[...].astype(F32) * inv_ref[...]
    cos = jnp.cos(ang)
    sin = jnp.sin(ang)
    tab_ref[0 * half:1 * half, :] = cos
    tab_ref[1 * half:2 * half, :] = cos
    tab_ref[2 * half:3 * half, :] = -sin
    tab_ref[3 * half:4 * half, :] = sin


def rope_table(positions, *, tn):
    n = positions.size
    half = QK_ROPE_DIM // 2
    inv_freq = ROPE_THETA ** (-jnp.arange(0, QK_ROPE_DIM, 2, dtype=F32) / QK_ROPE_DIM)
    return pl.pallas_call(
        _rope_kernel,
        grid=(n // tn,),
        in_specs=[pl.BlockSpec((half, 1), lambda i: (0, 0)),
                  pl.BlockSpec((1, tn), lambda i: (0, i))],
        out_specs=pl.BlockSpec((4 * half, tn), lambda i: (0, i)),
        out_shape=jax.ShapeDtypeStruct((4 * half, n), F32),
        compiler_params=_params("parallel"),
        name="rope_table",
    )(inv_freq.reshape(half, 1), positions.reshape(1, n))


def _ffn_ln_kernel(x_ref, wg_ref, wu_ref, wd_ref, g_ref, b_ref, *rest, ncast):
    cast_in, o_ref, cast_out, xb_ref = rest[:ncast], rest[ncast], rest[ncast + 1:-1], rest[-1]
    j = pl.program_id(1)

    @pl.when(j == 0)
    def _():
        xb_ref[...] = x_ref[...].astype(BF16)
        o_ref[...] = DEEPNORM_ALPHA * x_ref[...]

    xb = xb_ref[...]
    gate = _dot(xb, wg_ref[...])
    up = _dot(xb, wu_ref[...])
    h = (gate * jax.nn.sigmoid(gate) * up).astype(BF16)
    o_ref[...] += _dot(h, wd_ref[...])

    for idx, (src, dst) in enumerate(zip(cast_in, cast_out)):
        dst[...] = (src[...] * FFN_CAST_SCALES[idx]).astype(BF16)

    @pl.when(j == pl.num_programs(1) - 1)
    def _():
        def rows(c, _):
            r = pl.ds(pl.multiple_of(c * LN_ROWS, LN_ROWS), LN_ROWS)
            o_ref[r, :] = _layer_norm(o_ref[r, :], g_ref[...], b_ref[...])
            return 0

        lax.fori_loop(0, o_ref.shape[0] // LN_ROWS, rows, 0)


def ffn_ln(x, wg, wu, wd_half, g, b, *, tm, tf, cast=()):
    n, d = x.shape
    f = wg.shape[1]
    ni, nj = n // tm, f // tf
    assert len(cast) in (0, len(FFN_CAST_SCALES))
    up_spec = pl.BlockSpec((d // ni, tf), lambda i, j: (i, j))
    down_spec = pl.BlockSpec((f // (ni * nj), d), lambda i, j: (i * nj + j, 0))
    cast_specs = [up_spec, up_spec, down_spec][:len(cast)]
    out = pl.pallas_call(
        functools.partial(_ffn_ln_kernel, ncast=len(cast)),
        grid=(ni, nj),
        in_specs=[pl.BlockSpec((tm, d), lambda i, j: (i, 0)),
                  pl.BlockSpec((d, tf), lambda i, j: (0, j)),
                  pl.BlockSpec((d, tf), lambda i, j: (0, j)),
                  pl.BlockSpec((tf, d), lambda i, j: (j, 0)),
                  pl.BlockSpec((1, d), lambda i, j: (0, 0)),
                  pl.BlockSpec((1, d), lambda i, j: (0, 0))] + cast_specs,
        out_specs=[pl.BlockSpec((tm, d), lambda i, j: (i, 0))] + cast_specs,
        out_shape=[jax.ShapeDtypeStruct((n, d), F32)] + [jax.ShapeDtypeStruct(w.shape, BF16) for w in cast],
        scratch_shapes=[pltpu.VMEM((tm, d), BF16)],
        compiler_params=_params("parallel", "arbitrary"),
        name="ffn_ln",
    )(x, wg, wu, wd_half, g, b, *cast)
    return out[0], tuple(out[1:])


def _in_proj_kernel(x_ref, win_ref, wq_ref, wkv_ref, qg_ref, kvg_ref, rope_ref,
                    lru_ref, gate_ref, q_ref, k_ref, v_ref, *, lru_w, q_rank, kv_rank):
    xb = x_ref[...].astype(BF16)
    nope, rope, vd = QK_NOPE_DIM, QK_ROPE_DIM, V_HEAD_DIM
    tab = rope_ref[...].T

    def rotate(pe_and_swapped):
        prod = pe_and_swapped * tab
        return prod[:, :rope] + prod[:, rope:]

    lru_ref[...] = _dot(xb, win_ref[:, 0:lru_w])
    gate_ref[...] = jax.nn.gelu(_dot(xb, win_ref[:, lru_w:2 * lru_w])).astype(BF16)

    o = 2 * lru_w
    c_q = _dot(xb, win_ref[:, o:o + q_rank])
    o += q_rank
    c_kv = _dot(xb, win_ref[:, o:o + kv_rank])
    o += kv_rank
    kpe = _dot(xb, win_ref[:, o:o + rope])
    k_rot = rotate(jnp.concatenate([kpe, kpe[:, rope // 2:], kpe[:, :rope // 2]], -1))

    qn = _rms_norm(c_q, qg_ref[...]).astype(BF16)
    kvn = _rms_norm(c_kv, kvg_ref[...]).astype(BF16)
    qscale = (nope + rope) ** -0.5 * LOG2E
    hq = nope + 2 * rope
    for h in range(MLA_HEADS):
        qh = _dot(qn, wq_ref[:, h * hq:(h + 1) * hq])
        qrot = jnp.concatenate([qh[:, :nope], rotate(qh[:, nope:])], -1) * qscale
        q_ref[0, h, 0] = qrot.T.astype(BF16)
        kvh = _dot(kvn, wkv_ref[:, h * (nope + vd):(h + 1) * (nope + vd)])
        k_ref[0, h] = jnp.concatenate([kvh[:, :nope], k_rot], -1).astype(BF16)
        vt = kvh[:, nope:].T.astype(BF16)
        tk = v_ref.shape[-1]
        for c in range(v_ref.shape[2]):
            v_ref[0, h, c, 0:vd] = vt[:, c * tk:(c + 1) * tk]
            v_ref[0, h, c, vd:] = jnp.ones((v_ref.shape[3] - vd, tk), BF16)


def in_proj(x, w_in, w_q, w_kv, q_g, kv_g, rope_tab, *, batch, tm, lru_w):
    n, d = x.shape
    s = n // batch
    spb = s // tm
    q_rank, kv_rank = w_q.shape[0], w_kv.shape[0]
    qk = QK_NOPE_DIM + QK_ROPE_DIM
    tk = min(ATTN_TK, tm)
    const = lambda i: (0, 0)
    head_map = lambda i: (i // spb, 0, i % spb, 0)
    kern = functools.partial(_in_proj_kernel, lru_w=lru_w, q_rank=q_rank, kv_rank=kv_rank)
    return pl.pallas_call(
        kern,
        grid=(n // tm,),
        in_specs=[pl.BlockSpec((tm, d), lambda i: (i, 0)),
                  pl.BlockSpec(w_in.shape, const),
                  pl.BlockSpec(w_q.shape, const),
                  pl.BlockSpec(w_kv.shape, const),
                  pl.BlockSpec((1, q_rank), const),
                  pl.BlockSpec((1, kv_rank), const),
                  pl.BlockSpec((2 * QK_ROPE_DIM, tm), lambda i: (0, i))],
        out_specs=[pl.BlockSpec((tm, lru_w), lambda i: (i, 0)),
                   pl.BlockSpec((tm, lru_w), lambda i: (i, 0)),
                   pl.BlockSpec((1, MLA_HEADS, 1, qk, tm), lambda i: (i // spb, 0, i % spb, 0, 0)),
                   pl.BlockSpec((1, MLA_HEADS, tm, qk), head_map),
                   pl.BlockSpec((1, MLA_HEADS, tm // tk, V_HEAD_DIM + ATTN_ONES_ROWS, tk),
                                lambda i: (i // spb, 0, i % spb, 0, 0))],
        out_shape=[jax.ShapeDtypeStruct((n, lru_w), F32),
                   jax.ShapeDtypeStruct((n, lru_w), BF16),
                   jax.ShapeDtypeStruct((batch, MLA_HEADS, spb, qk, tm), BF16),
                   jax.ShapeDtypeStruct((batch, MLA_HEADS, s, qk), BF16),
                   jax.ShapeDtypeStruct((batch, MLA_HEADS, s // tk, V_HEAD_DIM + ATTN_ONES_ROWS, tk), BF16)],
        compiler_params=_params("parallel"),
        name="in_proj",
    )(x, w_in, w_q, w_kv, q_g, kv_g, rope_tab)


def _rglru_kernel(x_ref, gate_ref, cw_ref, cb_ref, wa_ref, ba_ref, wx_ref, bx_ref, lam_ref,
                  y_ref, xbuf, a_scr, u_scr, h_scr, *, ts, pad):
    t = pl.program_id(1)
    w = x_ref.shape[-1]

    @pl.when(t == 0)
    def _():
        xbuf[0:pad, :] = jnp.zeros((pad, w), F32)
        h_scr[0:1, :] = jnp.zeros((1, w), F32)

    xbuf[pad:pad + ts, :] = x_ref[0]
    xc = cb_ref[...] + cw_ref[CONV_WIDTH - 1:CONV_WIDTH, :] * xbuf[pad:pad + ts, :]
    for k in range(CONV_WIDTH - 1):
        sh = CONV_WIDTH - 1 - k
        xc = xc + cw_ref[k:k + 1, :] * xbuf[pad - sh:pad - sh + ts, :]
    xbuf[0:pad, :] = xbuf[ts:ts + pad, :]

    xcb = xc.astype(BF16)
    ngrp = w // MXU_DIM
    rec = jnp.concatenate(
        [_dot(xcb[:, g * MXU_DIM:(g + 1) * MXU_DIM], wa_ref[g]) for g in range(ngrp)], -1)
    ing = jnp.concatenate(
        [_dot(xcb[:, g * MXU_DIM:(g + 1) * MXU_DIM], wx_ref[g]) for g in range(ngrp)], -1)
    rec = _sigmoid(rec + ba_ref[...])
    ing = _sigmoid(ing + bx_ref[...])

    nlam = -lam_ref[...]
    softplus = jnp.maximum(nlam, 0.0) + jnp.log1p(jnp.exp(-jnp.abs(nlam)))
    log_a = -LRU_C * rec * softplus
    a = jnp.exp(log_a)
    mult = jnp.sqrt(jnp.tanh(-log_a) * (1.0 + a * a))
    a_scr[...] = a
    u_scr[...] = mult * (ing * xc)

    def step(i, h):
        r0, r1 = pl.ds(2 * i, 1), pl.ds(2 * i + 1, 1)
        a0, u0, a1, u1 = a_scr[r0, :], u_scr[r0, :], a_scr[r1, :], u_scr[r1, :]
        u_scr[r0, :] = a0 * h + u0
        h = (a1 * a0) * h + (a1 * u0 + u1)
        u_scr[r1, :] = h
        return h

    h_last = lax.fori_loop(0, ts // 2, step, h_scr[0:1, :], unroll=16)
    h_scr[0:1, :] = h_last
    y_ref[0] = (gate_ref[0].astype(F32) * u_scr[...]).astype(BF16)


def rglru(lru_in, gate, conv_w, conv_b, wa_bd, b_a, wx_bd, b_x, lam, *, ts):
    bsz, s, w = lru_in.shape
    pad = 8
    const2 = lambda b, t: (0, 0)
    const3 = lambda b, t: (0, 0, 0)
    kern = functools.partial(_rglru_kernel, ts=ts, pad=pad)
    return pl.pallas_call(
        kern,
        grid=(bsz, s // ts),
        in_specs=[pl.BlockSpec((1, ts, w), lambda b, t: (b, t, 0)),
                  pl.BlockSpec((1, ts, w), lambda b, t: (b, t, 0)),
                  pl.BlockSpec(conv_w.shape, const2),
                  pl.BlockSpec((1, w), const2),
                  pl.BlockSpec(wa_bd.shape, const3),
                  pl.BlockSpec((1, w), const2),
                  pl.BlockSpec(wx_bd.shape, const3),
                  pl.BlockSpec((1, w), const2),
                  pl.BlockSpec((1, w), const2)],
        out_specs=pl.BlockSpec((1, ts, w), lambda b, t: (b, t, 0)),
        out_shape=jax.ShapeDtypeStruct((bsz, s, w), BF16),
        scratch_shapes=[pltpu.VMEM((ts + pad, w), F32),
                        pltpu.VMEM((ts, w), F32),
                        pltpu.VMEM((ts, w), F32),
                        pltpu.VMEM((8, w), F32)],
        compiler_params=_params("parallel", "arbitrary"),
        name="rglru",
    )(lru_in, gate, conv_w, conv_b, wa_bd, b_a, wx_bd, b_x, lam)


def _block_diag_groups(wblk):
    g, d, _ = wblk.shape
    per = MXU_DIM // d
    wg = wblk.reshape(g // per, per, d, d)
    eye = jnp.eye(per, dtype=wblk.dtype)
    out = jnp.einsum('gpij,pq->gpiqj', wg, eye)
    return out.reshape(g // per, MXU_DIM, MXU_DIM)


HEADS_PER_STEP = 2
ATTN_TK = 256
ATTN_ONES_ROWS = 16
ATTN_KV_UNROLL = 2


def _attn_kernel(q_ref, k_ref, vt_ref, o_ref, *scratch, tq):
    nh, nq = q_ref.shape[0], q_ref.shape[1]
    dva, tk = vt_ref.shape[2], vt_ref.shape[3]
    dv = dva - ATTN_ONES_ROWS
    sub = tq // tk
    assert sub % 2 == 0
    heads = range(nh)
    st_scr = [scratch[2 * hh:2 * hh + 2] for hh in heads]
    acc_scr = scratch[2 * nh:]

    def scores(qs, kblk, slot):
        for hh in heads:
            st_scr[hh][slot][...] = _dot(k_ref[hh, pl.ds(kblk * tk, tk), :], qs[hh])

    def update(hh, kblk, slot, diag_off, m):
        st = st_scr[hh][slot][...]
        if diag_off is not None:
            key = lax.broadcasted_iota(jnp.int32, (tk, tq), 0) + diag_off
            qry = lax.broadcasted_iota(jnp.int32, (tk, tq), 1)
            st = jnp.where(key <= qry, st, -jnp.inf)
        m_new = jnp.maximum(m, jnp.max(st, axis=0, keepdims=True))
        pt = jnp.exp2(st - m_new).astype(BF16)
        acc_scr[hh][...] = (jnp.exp2(m - m_new) * acc_scr[hh][...]
                            + _dot(vt_ref[hh, kblk], pt))
        return m_new

    def block(kblk, slot, diag_off, ms, ahead):
        scores(*ahead, 1 - slot)
        return tuple(update(hh, kblk, slot, diag_off, ms[hh]) for hh in heads)

    def load_q(qi):
        return [q_ref[hh, qi] for hh in heads]

    scores(load_q(0), 0, 0)

    def q_tile(qi, _):
        qs = load_q(qi)
        for hh in heads:
            acc_scr[hh][...] = jnp.zeros((dva, tq), F32)

        def kv_tiles(ntiles, kj, ms):
            for c in range(ntiles * sub):
                ms = block(kj * sub + c, c % 2, None, ms, (qs, kj * sub + c + 1))
            return ms

        n_main = qi // ATTN_KV_UNROLL
        ms = tuple(jnp.full((1, tq), -jnp.inf, F32) for _ in heads)
        ms = lax.fori_loop(0, n_main,
                           lambda it, ms: kv_tiles(ATTN_KV_UNROLL, it * ATTN_KV_UNROLL, ms), ms)
        ms = lax.fori_loop(n_main * ATTN_KV_UNROLL, qi, functools.partial(kv_tiles, 1), ms)
        q_next = load_q(jnp.minimum(qi + 1, nq - 1))
        for c in range(sub):
            ahead = (qs, qi * sub + c + 1) if c + 1 < sub else (q_next, 0)
            ms = block(qi * sub + c, c % 2, c * tk, ms, ahead)
        for hh in heads:
            out = acc_scr[hh][0:dv, :] / acc_scr[hh][dv:dv + 1, :]
            o_ref[pl.ds(qi * tq, tq), hh * dv:(hh + 1) * dv] = out.T.astype(o_ref.dtype)
        return 0

    lax.fori_loop(0, nq, q_tile, 0)


def attention(qt, k, vt):
    bsz, nh, nq, dqk, tq = qt.shape
    s = nq * tq
    _, _, nkb, dva, tk = vt.shape
    dv = dva - ATTN_ONES_ROWS
    hps = HEADS_PER_STEP
    return pl.pallas_call(
        functools.partial(_attn_kernel, tq=tq),
        grid=(bsz, nh // hps),
        in_specs=[pl.BlockSpec((None, hps, nq, dqk, tq), lambda b, h: (b, h, 0, 0, 0)),
                  pl.BlockSpec((None, hps, s, dqk), lambda b, h: (b, h, 0, 0)),
                  pl.BlockSpec((None, hps, nkb, dva, tk), lambda b, h: (b, h, 0, 0, 0))],
        out_specs=pl.BlockSpec((None, s, hps * dv), lambda b, h: (b, 0, h)),
        out_shape=jax.ShapeDtypeStruct((bsz, s, nh * dv), BF16),
        scratch_shapes=([pltpu.VMEM((tk, tq), F32)] * (2 * hps) + [pltpu.VMEM((dva, tq), F32)] * hps),
        compiler_params=_params("parallel", "parallel"),
        name="mla_attention",
    )(qt, k, vt)


def _row_chunks(nrows):
    step = min(MM_ROWS, nrows)
    return [slice(r, r + step) for r in range(0, nrows, step)]


def _out_proj_kernel(x_ref, yl_ref, ym_ref, w_ref, g_ref, b_ref, o_ref):
    wl = yl_ref.shape[1]
    for r in _row_chunks(x_ref.shape[0]):
        mix = _dot(yl_ref[r, :], w_ref[0:wl, :]) + _dot(ym_ref[r, :], w_ref[wl:, :])
        o_ref[r, :] = _layer_norm(DEEPNORM_ALPHA * x_ref[r, :] + mix, g_ref[...], b_ref[...])


def out_proj_ln(x, y_lru, y_mla, w_out, g, b, *, tm):
    n, d = x.shape
    wl, wm = y_lru.shape[1], y_mla.shape[1]
    const = lambda i: (0, 0)
    return pl.pallas_call(
        _out_proj_kernel,
        grid=(n // tm,),
        in_specs=[pl.BlockSpec((tm, d), lambda i: (i, 0)),
                  pl.BlockSpec((tm, wl), lambda i: (i, 0)),
                  pl.BlockSpec((tm, wm), lambda i: (i, 0)),
                  pl.BlockSpec((wl + wm, d), const),
                  pl.BlockSpec((1, d), const),
                  pl.BlockSpec((1, d), const)],
        out_specs=pl.BlockSpec((tm, d), lambda i: (i, 0)),
        out_shape=jax.ShapeDtypeStruct((n, d), F32),
        compiler_params=_params("parallel"),
        name="out_proj_ln",
    )(x, y_lru, y_mla, w_out, g, b)


def _ple_kernel(x_ref, p_ref, wg_ref, bg_ref, wp_ref, g_ref, b_ref, o_ref):
    for r in _row_chunks(x_ref.shape[0]):
        x = x_ref[r, :]
        gate = _sigmoid(_dot(x.astype(BF16), wg_ref[...]) + bg_ref[...])
        proj = _dot(p_ref[r, :].astype(BF16), wp_ref[...])
        o_ref[r, :] = _layer_norm(DEEPNORM_ALPHA * x + gate * proj, g_ref[...], b_ref[...])


def ple_ln(x, p, w_gate, b_gate, w_proj, g, b, *, tm):
    n, d = x.shape
    pd = p.shape[1]
    const = lambda i: (0, 0)
    return pl.pallas_call(
        _ple_kernel,
        grid=(n // tm,),
        in_specs=[pl.BlockSpec((tm, d), lambda i: (i, 0)),
                  pl.BlockSpec((tm, pd), lambda i: (i, 0)),
                  pl.BlockSpec((d, d), const),
                  pl.BlockSpec((1, d), const),
                  pl.BlockSpec((pd, d), const),
                  pl.BlockSpec((1, d), const),
                  pl.BlockSpec((1, d), const)],
        out_specs=pl.BlockSpec((tm, d), lambda i: (i, 0)),
        out_shape=jax.ShapeDtypeStruct((n, d), F32),
        compiler_params=_params("parallel"),
        name="ple_ln",
    )(x, p, w_gate, b_gate, w_proj, g, b)


def _swap_halves_cols(w):
    half = w.shape[-1] // 2
    return jnp.concatenate([w[..., half:], w[..., :half]], -1)


def _tile(n, pref):
    t = min(n, pref)
    assert n % t == 0
    return t


def _layer(x, p, rope_tab, prm, *, batch):
    n, d = x.shape
    row = lambda v: v.reshape(1, -1)
    tm = _tile(n // batch, 512)
    tm_ffn = _tile(n, 1024)

    tf = _tile(prm["ffn1_w_gate"].shape[1], 512)
    half = FFN_CAST_SCALES[2]
    x, ffn2_w = ffn_ln(x, prm["ffn1_w_gate"].astype(BF16), prm["ffn1_w_up"].astype(BF16),
                       (half * prm["ffn1_w_down"]).astype(BF16), row(prm["ln1_g"]), row(prm["ln1_b"]),
                       tm=tm_ffn, tf=tf,
                       cast=(prm["ffn2_w_gate"], prm["ffn2_w_up"], prm["ffn2_w_down"]))

    lru_w = prm["conv_w"].shape[1]
    q_rank = prm["w_q_up"].shape[0]
    wq = prm["w_q_up"].astype(BF16).reshape(q_rank, MLA_HEADS, QK_NOPE_DIM + QK_ROPE_DIM)
    wq_ext = jnp.concatenate([wq, _swap_halves_cols(wq[..., QK_NOPE_DIM:])], -1)
    wq_ext = wq_ext.reshape(q_rank, -1)
    lru_in, gate, q, k, vt = in_proj(
        x, prm["w_in"].astype(BF16), wq_ext, prm["w_kv_up"].astype(BF16), row(prm["q_norm_g"]),
        row(prm["kv_norm_g"]), rope_tab, batch=batch, tm=tm, lru_w=lru_w)

    s = n // batch
    y_lru = rglru(lru_in.reshape(batch, s, lru_w), gate.reshape(batch, s, lru_w),
                  prm["conv_w"], row(prm["conv_b"]),
                  _block_diag_groups(prm["lru_w_a"]).astype(BF16), row(prm["lru_b_a"]),
                  _block_diag_groups(prm["lru_w_x"]).astype(BF16), row(prm["lru_b_x"]),
                  row(prm["lru_lambda"]), ts=_tile(s, 512))
    y_mla = attention(q, k, vt)

    x = out_proj_ln(x, y_lru.reshape(n, lru_w), y_mla.reshape(n, -1), prm["w_out"].astype(BF16),
                    row(prm["ln2_g"]), row(prm["ln2_b"]), tm=tm)

    x, _ = ffn_ln(x, *ffn2_w, row(prm["ln3_g"]), row(prm["ln3_b"]), tm=tm_ffn, tf=tf)

    return ple_ln(x, p, prm["ple_w_gate"].astype(BF16), row(prm["ple_b_gate"]),
                  prm["ple_w_proj"].astype(BF16), row(prm["ln4_g"]), row(prm["ln4_b"]), tm=tm)


def kernel(x, p, positions, ffn1_w_gate, ffn1_w_up, ffn1_w_down, ln1_g, ln1_b, w_in, conv_w, conv_b, lru_w_a, lru_b_a, lru_w_x, lru_b_x, lru_lambda, q_norm_g, w_q_up, kv_norm_g, w_kv_up, w_out, ln2_g, ln2_b, ffn2_w_gate, ffn2_w_up, ffn2_w_down, ln3_g, ln3_b, ple_w_gate, ple_b_gate, ple_w_proj, ln4_g, ln4_b):
    names = ("ffn1_w_gate ffn1_w_up ffn1_w_down ln1_g ln1_b w_in conv_w conv_b lru_w_a lru_b_a lru_w_x "
             "lru_b_x lru_lambda q_norm_g w_q_up kv_norm_g w_kv_up w_out ln2_g ln2_b ffn2_w_gate ffn2_w_up "
             "ffn2_w_down ln3_g ln3_b ple_w_gate ple_b_gate ple_w_proj ln4_g ln4_b").split()
    stacked = dict(zip(names, (ffn1_w_gate, ffn1_w_up, ffn1_w_down, ln1_g, ln1_b, w_in, conv_w, conv_b,
                               lru_w_a, lru_b_a, lru_w_x, lru_b_x, lru_lambda, q_norm_g, w_q_up, kv_norm_g,
                               w_kv_up, w_out, ln2_g, ln2_b, ffn2_w_gate, ffn2_w_up, ffn2_w_down, ln3_g,
                               ln3_b, ple_w_gate, ple_b_gate, ple_w_proj, ln4_g, ln4_b)))
    batch, seq, d = x.shape
    rope_tab = rope_table(positions, tn=_tile(batch * seq, 4096))
    h = x.reshape(batch * seq, d)
    assert ffn1_w_gate.shape[0] == DEPTH
    for i in range(DEPTH):
        prm = {kname: val[i] for kname, val in stacked.items()}
        h = _layer(h, p[i].reshape(batch * seq, -1), rope_tab, prm, batch=batch)
    return h.reshape(batch, seq, d)
```

```python
import functools
import math

import jax
import jax.numpy as jnp
import numpy as np
from jax import lax
from jax.experimental import pallas as pl
from jax.experimental.pallas import tpu as pltpu

LRU_BLOCKS = 16
CONV_WIDTH = 4
LRU_C = 8.0
MLA_HEADS = 8
QK_NOPE_DIM = 128
QK_ROPE_DIM = 64
V_HEAD_DIM = 128
ROPE_THETA = 10000.0
LN_EPS = 1e-5
RMS_EPS = 1e-6
DEPTH = 1
DEEPNORM_ALPHA = (2 * DEPTH) ** 0.25

LANES = 128
SUBLANES = 8
MXU_DIM = 256
VMEM_LIMIT = 61 * 1024 * 1024
LOG2E = math.log2(math.e)
LN_ROWS = 256
MM_ROWS = 256
SCAN_ROWS = 32
FFN_CAST_SCALES = (1.0, 1.0, 0.5)

BF16 = jnp.bfloat16
F32 = jnp.float32


def _params(*sem):
    return pltpu.CompilerParams(dimension_semantics=sem, vmem_limit_bytes=VMEM_LIMIT)


def _layer_norm(y, g, b):
    mu = jnp.mean(y, axis=-1, keepdims=True)
    yc = y - mu
    var = jnp.mean(yc * yc, axis=-1, keepdims=True)
    return yc * lax.rsqrt(var + LN_EPS) * g + b


def _rms_norm(y, g):
    return y * lax.rsqrt(jnp.mean(y * y, axis=-1, keepdims=True) + RMS_EPS) * g


def _dot(a, b):
    return jnp.dot(a, b, preferred_element_type=F32)


def _sigmoid(x):
    return 0.5 * jnp.tanh(0.5 * x) + 0.5


def _rope_kernel(inv_ref, pos_ref, tab_ref):
    half = inv_ref.shape[0]
    ang = pos_ref[...].astype(F32) * inv_ref[...]
    cos = jnp.cos(ang)
    sin = jnp.sin(ang)
    tab_ref[0 * half:1 * half, :] = cos
    tab_ref[1 * half:2 * half, :] = cos
    tab_ref[2 * half:3 * half, :] = -sin
    tab_ref[3 * half:4 * half, :] = sin


def rope_table(positions, *, tn):
    n = positions.size
    half = QK_ROPE_DIM // 2
    inv_freq = ROPE_THETA ** (-jnp.arange(0, QK_ROPE_DIM, 2, dtype=F32) / QK_ROPE_DIM)
    return pl.pallas_call(
        _rope_kernel,
        grid=(n // tn,),
        in_specs=[pl.BlockSpec((half, 1), lambda i: (0, 0)),
                  pl.BlockSpec((1, tn), lambda i: (0, i))],
        out_specs=pl.BlockSpec((4 * half, tn), lambda i: (0, i)),
        out_shape=jax.ShapeDtypeStruct((4 * half, n), F32),
        compiler_params=_params("parallel"),
        name="rope_table",
    )(inv_freq.reshape(half, 1), positions.reshape(1, n))


def _ffn_ln_kernel(x_ref, wg_ref, wu_ref, wd_ref, g_ref, b_ref, *rest, ncast):
    cast_in, o_ref, cast_out, xb_ref = rest[:ncast], rest[ncast], rest[ncast + 1:-1], rest[-1]
    j = pl.program_id(1)

    @pl.when(j == 0)
    def _():
        xb_ref[...] = x_ref[...].astype(BF16)
        o_ref[...] = DEEPNORM_ALPHA * x_ref[...]

    xb = xb_ref[...]
    gate = _dot(xb, wg_ref[...])
    up = _dot(xb, wu_ref[...])
    h = (gate * jax.nn.sigmoid(gate) * up).astype(BF16)
    o_ref[...] += _dot(h, wd_ref[...])

    for idx, (src, dst) in enumerate(zip(cast_in, cast_out)):
        dst[...] = (src[...] * FFN_CAST_SCALES[idx]).astype(BF16)

    @pl.when(j == pl.num_programs(1) - 1)
    def _():
        def rows(c, _):
            r = pl.ds(pl.multiple_of(c * LN_ROWS, LN_ROWS), LN_ROWS)
            o_ref[r, :] = _layer_norm(o_ref[r, :], g_ref[...], b_ref[...])
            return 0

        lax.fori_loop(0, o_ref.shape[0] // LN_ROWS, rows, 0)


def ffn_ln(x, wg, wu, wd_half, g, b, *, tm, tf, cast=()):
    n, d = x.shape
    f = wg.shape[1]
    ni, nj = n // tm, f // tf
    assert len(cast) in (0, len(FFN_CAST_SCALES))
    up_spec = pl.BlockSpec((d // ni, tf), lambda i, j: (i, j))
    down_spec = pl.BlockSpec((f // (ni * nj), d), lambda i, j: (i * nj + j, 0))
    cast_specs = [up_spec, up_spec, down_spec][:len(cast)]
    out = pl.pallas_call(
        functools.partial(_ffn_ln_kernel, ncast=len(cast)),
        grid=(ni, nj),
        in_specs=[pl.BlockSpec((tm, d), lambda i, j: (i, 0)),
                  pl.BlockSpec((d, tf), lambda i, j: (0, j)),
                  pl.BlockSpec((d, tf), lambda i, j: (0, j)),
                  pl.BlockSpec((tf, d), lambda i, j: (j, 0)),
                  pl.BlockSpec((1, d), lambda i, j: (0, 0)),
                  pl.BlockSpec((1, d), lambda i, j: (0, 0))] + cast_specs,
        out_specs=[pl.BlockSpec((tm, d), lambda i, j: (i, 0))] + cast_specs,
        out_shape=[jax.ShapeDtypeStruct((n, d), F32)] + [jax.ShapeDtypeStruct(w.shape, BF16) for w in cast],
        scratch_shapes=[pltpu.VMEM((tm, d), BF16)],
        compiler_params=_params("parallel", "arbitrary"),
        name="ffn_ln",
    )(x, wg, wu, wd_half, g, b, *cast)
    return out[0], tuple(out[1:])


def _in_proj_kernel(x_ref, win_ref, wq_ref, wkv_ref, qg_ref, kvg_ref, rope_ref,
                    lru_ref, gate_ref, q_ref, k_ref, v_ref, *, lru_w, q_rank, kv_rank):
    xb = x_ref[...].astype(BF16)
    nope, rope, vd = QK_NOPE_DIM, QK_ROPE_DIM, V_HEAD_DIM
    tab = rope_ref[...].T

    def rotate(pe_and_swapped):
        prod = pe_and_swapped * tab
        return prod[:, :rope] + prod[:, rope:]

    lru_ref[...] = _dot(xb, win_ref[:, 0:lru_w])
    gate_ref[...] = jax.nn.gelu(_dot(xb, win_ref[:, lru_w:2 * lru_w])).astype(BF16)

    o = 2 * lru_w
    c_q = _dot(xb, win_ref[:, o:o + q_rank])
    o += q_rank
    c_kv = _dot(xb, win_ref[:, o:o + kv_rank])
    o += kv_rank
    kpe = _dot(xb, win_ref[:, o:o + rope])
    k_rot = rotate(jnp.concatenate([kpe, kpe[:, rope // 2:], kpe[:, :rope // 2]], -1))

    qn = _rms_norm(c_q, qg_ref[...]).astype(BF16)
    kvn = _rms_norm(c_kv, kvg_ref[...]).astype(BF16)
    qscale = (nope + rope) ** -0.5 * LOG2E
    hq = nope + 2 * rope
    for h in range(MLA_HEADS):
        qh = _dot(qn, wq_ref[:, h * hq:(h + 1) * hq])
        qrot = jnp.concatenate([qh[:, :nope], rotate(qh[:, nope:])], -1) * qscale
        q_ref[0, h, 0] = qrot.T.astype(BF16)
        kvh = _dot(kvn, wkv_ref[:, h * (nope + vd):(h + 1) * (nope + vd)])
        k_ref[0, h] = jnp.concatenate([kvh[:, :nope], k_rot], -1).astype(BF16)
        vt = kvh[:, nope:].T.astype(BF16)
        tk = v_ref.shape[-1]
        for c in range(v_ref.shape[2]):
            v_ref[0, h, c, 0:vd] = vt[:, c * tk:(c + 1) * tk]
            v_ref[0, h, c, vd:] = jnp.ones((v_ref.shape[3] - vd, tk), BF16)


def in_proj(x, w_in, w_q, w_kv, q_g, kv_g, rope_tab, *, batch, tm, lru_w):
    n, d = x.shape
    s = n // batch
    spb = s // tm
    q_rank, kv_rank = w_q.shape[0], w_kv.shape[0]
    qk = QK_NOPE_DIM + QK_ROPE_DIM
    tk = min(ATTN_TK, tm)
    const = lambda i: (0, 0)
    head_map = lambda i: (i // spb, 0, i % spb, 0)
    kern = functools.partial(_in_proj_kernel, lru_w=lru_w, q_rank=q_rank, kv_rank=kv_rank)
    return pl.pallas_call(
        kern,
        grid=(n // tm,),
        in_specs=[pl.BlockSpec((tm, d), lambda i: (i, 0)),
                  pl.BlockSpec(w_in.shape, const),
                  pl.BlockSpec(w_q.shape, const),
                  pl.BlockSpec(w_kv.shape, const),
                  pl.BlockSpec((1, q_rank), const),
                  pl.BlockSpec((1, kv_rank), const),
                  pl.BlockSpec((2 * QK_ROPE_DIM, tm), lambda i: (0, i))],
        out_specs=[pl.BlockSpec((tm, lru_w), lambda i: (i, 0)),
                   pl.BlockSpec((tm, lru_w), lambda i: (i, 0)),
                   pl.BlockSpec((1, MLA_HEADS, 1, qk, tm), lambda i: (i // spb, 0, i % spb, 0, 0)),
                   pl.BlockSpec((1, MLA_HEADS, tm, qk), head_map),
                   pl.BlockSpec((1, MLA_HEADS, tm // tk, V_HEAD_DIM + ATTN_ONES_ROWS, tk),
                                lambda i: (i // spb, 0, i % spb, 0, 0))],
        out_shape=[jax.ShapeDtypeStruct((n, lru_w), F32),
                   jax.ShapeDtypeStruct((n, lru_w), BF16),
                   jax.ShapeDtypeStruct((batch, MLA_HEADS, spb, qk, tm), BF16),
                   jax.ShapeDtypeStruct((batch, MLA_HEADS, s, qk), BF16),
                   jax.ShapeDtypeStruct((batch, MLA_HEADS, s // tk, V_HEAD_DIM + ATTN_ONES_ROWS, tk), BF16)],
        compiler_params=_params("parallel"),
        name="in_proj",
    )(x, w_in, w_q, w_kv, q_g, kv_g, rope_tab)


def _rglru_kernel(x_ref, gate_ref, cw_ref, cb_ref, wa_ref, ba_ref, wx_ref, bx_ref, lam_ref,
                  y_ref, xbuf, a_scr, u_scr, hout_scr, h_scr, *, ts, pad):
    t = pl.program_id(1)
    w = x_ref.shape[-1]

    @pl.when(t == 0)
    def _():
        xbuf[0:pad, :] = jnp.zeros((pad, w), F32)
        h_scr[0:1, :] = jnp.zeros((1, w), F32)

    xbuf[pad:pad + ts, :] = x_ref[0]
    xc = cb_ref[...] + cw_ref[CONV_WIDTH - 1:CONV_WIDTH, :] * xbuf[pad:pad + ts, :]
    for k in range(CONV_WIDTH - 1):
        sh = CONV_WIDTH - 1 - k
        xc = xc + cw_ref[k:k + 1, :] * xbuf[pad - sh:pad - sh + ts, :]
    xbuf[0:pad, :] = xbuf[ts:ts + pad, :]

    xcb = xc.astype(BF16)
    ngrp = w // MXU_DIM
    rec = jnp.concatenate(
        [_dot(xcb[:, g * MXU_DIM:(g + 1) * MXU_DIM], wa_ref[g]) for g in range(ngrp)], -1)
    ing = jnp.concatenate(
        [_dot(xcb[:, g * MXU_DIM:(g + 1) * MXU_DIM], wx_ref[g]) for g in range(ngrp)], -1)
    rec = _sigmoid(rec + ba_ref[...])
    ing = _sigmoid(ing + bx_ref[...])

    nlam = -lam_ref[...]
    softplus = jnp.maximum(nlam, 0.0) + jnp.log1p(jnp.exp(-jnp.abs(nlam)))
    log_a = -LRU_C * rec * softplus
    a = jnp.exp(log_a)
    mult = jnp.sqrt(jnp.tanh(-log_a) * (1.0 + a * a))
    a_scr[...] = a.reshape(a_scr.shape)
    u_scr[...] = (mult * (ing * xc)).reshape(u_scr.shape)

    def group(g, h):
        for k in range(0, SCAN_ROWS, 2):
            t0, t1 = g * (SCAN_ROWS // SUBLANES) + k // SUBLANES, g * (SCAN_ROWS // SUBLANES) + (k + 1) // SUBLANES
            r0, r1 = pl.ds(k % SUBLANES, 1), pl.ds((k + 1) % SUBLANES, 1)
            a0, u0, a1, u1 = a_scr[t0, r0, :], u_scr[t0, r0, :], a_scr[t1, r1, :], u_scr[t1, r1, :]
            hout_scr[t0, r0, :] = a0 * h + u0
            h = (a1 * a0) * h + (a1 * u0 + u1)
            hout_scr[t1, r1, :] = h
        return h

    h_last = lax.fori_loop(0, ts // SCAN_ROWS, group, h_scr[0:1, :])
    h_scr[0:1, :] = h_last
    y_ref[0] = (gate_ref[0].astype(F32) * hout_scr[...].reshape(ts, w)).astype(BF16)


def rglru(lru_in, gate, conv_w, conv_b, wa_bd, b_a, wx_bd, b_x, lam, *, ts):
    bsz, s, w = lru_in.shape
    pad = 8
    const2 = lambda b, t: (0, 0)
    const3 = lambda b, t: (0, 0, 0)
    kern = functools.partial(_rglru_kernel, ts=ts, pad=pad)
    return pl.pallas_call(
        kern,
        grid=(bsz, s // ts),
        in_specs=[pl.BlockSpec((1, ts, w), lambda b, t: (b, t, 0)),
                  pl.BlockSpec((1, ts, w), lambda b, t: (b, t, 0)),
                  pl.BlockSpec(conv_w.shape, const2),
                  pl.BlockSpec((1, w), const2),
                  pl.BlockSpec(wa_bd.shape, const3),
                  pl.BlockSpec((1, w), const2),
                  pl.BlockSpec(wx_bd.shape, const3),
                  pl.BlockSpec((1, w), const2),
                  pl.BlockSpec((1, w), const2)],
        out_specs=pl.BlockSpec((1, ts, w), lambda b, t: (b, t, 0)),
        out_shape=jax.ShapeDtypeStruct((bsz, s, w), BF16),
        scratch_shapes=[pltpu.VMEM((ts + pad, w), F32),
                        pltpu.VMEM((ts // SUBLANES, SUBLANES, w), F32),
                        pltpu.VMEM((ts // SUBLANES, SUBLANES, w), F32),
                        pltpu.VMEM((ts // SUBLANES, SUBLANES, w), F32),
                        pltpu.VMEM((8, w), F32)],
        compiler_params=_params("parallel", "arbitrary"),
        name="rglru",
    )(lru_in, gate, conv_w, conv_b, wa_bd, b_a, wx_bd, b_x, lam)


def _block_diag_groups(wblk):
    g, d, _ = wblk.shape
    per = MXU_DIM // d
    wg = wblk.reshape(g // per, per, d, d)
    eye = jnp.eye(per, dtype=wblk.dtype)
    out = jnp.einsum('gpij,pq->gpiqj', wg, eye)
    return out.reshape(g // per, MXU_DIM, MXU_DIM)


HEADS_PER_STEP = 2
ATTN_TK = 256
ATTN_ONES_ROWS = 16
ATTN_KV_UNROLL = 2


def _attn_kernel(q_ref, k_ref, vt_ref, o_ref, *scratch, tq):
    nh, nq = q_ref.shape[0], q_ref.shape[1]
    dva, tk = vt_ref.shape[2], vt_ref.shape[3]
    dv = dva - ATTN_ONES_ROWS
    sub = tq // tk
    assert sub % 2 == 0
    heads = range(nh)
    st_scr = [scratch[2 * hh:2 * hh + 2] for hh in heads]
    acc_scr = scratch[2 * nh:]

    def scores(qs, kblk, slot):
        for hh in heads:
            st_scr[hh][slot][...] = _dot(k_ref[hh, pl.ds(kblk * tk, tk), :], qs[hh])

    def update(hh, kblk, slot, diag_off, m):
        st = st_scr[hh][slot][...]
        if diag_off is not None:
            key = lax.broadcasted_iota(jnp.int32, (tk, tq), 0) + diag_off
            qry = lax.broadcasted_iota(jnp.int32, (tk, tq), 1)
            st = jnp.where(key <= qry, st, -jnp.inf)
        m_new = jnp.maximum(m, jnp.max(st, axis=0, keepdims=True))
        pt = jnp.exp2(st - m_new).astype(BF16)
        acc_scr[hh][...] = (jnp.exp2(m - m_new) * acc_scr[hh][...]
                            + _dot(vt_ref[hh, kblk], pt))
        return m_new

    def block(kblk, slot, diag_off, ms, ahead):
        scores(*ahead, 1 - slot)
        return tuple(update(hh, kblk, slot, diag_off, ms[hh]) for hh in heads)

    def load_q(qi):
        return [q_ref[hh, qi] for hh in heads]

    scores(load_q(0), 0, 0)

    def q_tile(qi, _):
        qs = load_q(qi)
        for hh in heads:
            acc_scr[hh][...] = jnp.zeros((dva, tq), F32)

        def kv_tiles(ntiles, kj, ms):
            for c in range(ntiles * sub):
                ms = block(kj * sub + c, c % 2, None, ms, (qs, kj * sub + c + 1))
            return ms

        n_main = qi // ATTN_KV_UNROLL
        ms = tuple(jnp.full((1, tq), -jnp.inf, F32) for _ in heads)
        ms = lax.fori_loop(0, n_main,
                           lambda it, ms: kv_tiles(ATTN_KV_UNROLL, it * ATTN_KV_UNROLL, ms), ms)
        ms = lax.fori_loop(n_main * ATTN_KV_UNROLL, qi, functools.partial(kv_tiles, 1), ms)
        q_next = load_q(jnp.minimum(qi + 1, nq - 1))
        for c in range(sub):
            ahead = (qs, qi * sub + c + 1) if c + 1 < sub else (q_next, 0)
            ms = block(qi * sub + c, c % 2, c * tk, ms, ahead)
        for hh in heads:
            out = acc_scr[hh][0:dv, :] / acc_scr[hh][dv:dv + 1, :]
            o_ref[pl.ds(qi * tq, tq), hh * dv:(hh + 1) * dv] = out.T.astype(o_ref.dtype)
        return 0

    lax.fori_loop(0, nq, q_tile, 0)


def attention(qt, k, vt):
    bsz, nh, nq, dqk, tq = qt.shape
    s = nq * tq
    _, _, nkb, dva, tk = vt.shape
    dv = dva - ATTN_ONES_ROWS
    hps = HEADS_PER_STEP
    return pl.pallas_call(
        functools.partial(_attn_kernel, tq=tq),
        grid=(bsz, nh // hps),
        in_specs=[pl.BlockSpec((None, hps, nq, dqk, tq), lambda b, h: (b, h, 0, 0, 0)),
                  pl.BlockSpec((None, hps, s, dqk), lambda b, h: (b, h, 0, 0)),
                  pl.BlockSpec((None, hps, nkb, dva, tk), lambda b, h: (b, h, 0, 0, 0))],
        out_specs=pl.BlockSpec((None, s, hps * dv), lambda b, h: (b, 0, h)),
        out_shape=jax.ShapeDtypeStruct((bsz, s, nh * dv), BF16),
        scratch_shapes=([pltpu.VMEM((tk, tq), F32)] * (2 * hps) + [pltpu.VMEM((dva, tq), F32)] * hps),
        compiler_params=_params("parallel", "parallel"),
        name="mla_attention",
    )(qt, k, vt)


def _row_chunks(nrows):
    step = min(MM_ROWS, nrows)
    return [slice(r, r + step) for r in range(0, nrows, step)]


def _out_proj_kernel(x_ref, yl_ref, ym_ref, w_ref, g_ref, b_ref, o_ref):
    wl = yl_ref.shape[1]
    for r in _row_chunks(x_ref.shape[0]):
        mix = _dot(yl_ref[r, :], w_ref[0:wl, :]) + _dot(ym_ref[r, :], w_ref[wl:, :])
        o_ref[r, :] = _layer_norm(DEEPNORM_ALPHA * x_ref[r, :] + mix, g_ref[...], b_ref[...])


def out_proj_ln(x, y_lru, y_mla, w_out, g, b, *, tm):
    n, d = x.shape
    wl, wm = y_lru.shape[1], y_mla.shape[1]
    const = lambda i: (0, 0)
    return pl.pallas_call(
        _out_proj_kernel,
        grid=(n // tm,),
        in_specs=[pl.BlockSpec((tm, d), lambda i: (i, 0)),
                  pl.BlockSpec((tm, wl), lambda i: (i, 0)),
                  pl.BlockSpec((tm, wm), lambda i: (i, 0)),
                  pl.BlockSpec((wl + wm, d), const),
                  pl.BlockSpec((1, d), const),
                  pl.BlockSpec((1, d), const)],
        out_specs=pl.BlockSpec((tm, d), lambda i: (i, 0)),
        out_shape=jax.ShapeDtypeStruct((n, d), F32),
        compiler_params=_params("parallel"),
        name="out_proj_ln",
    )(x, y_lru, y_mla, w_out, g, b)


def _ple_kernel(x_ref, p_ref, wg_ref, bg_ref, wp_ref, g_ref, b_ref, o_ref):
    for r in _row_chunks(x_ref.shape[0]):
        x = x_ref[r, :]
        gate = _sigmoid(_dot(x.astype(BF16), wg_ref[...]) + bg_ref[...])
        proj = _dot(p_ref[r, :].astype(BF16), wp_ref[...])
        o_ref[r, :] = _layer_norm(DEEPNORM_ALPHA * x + gate * proj, g_ref[...], b_ref[...])


def ple_ln(x, p, w_gate, b_gate, w_proj, g, b, *, tm):
    n, d = x.shape
    pd = p.shape[1]
    const = lambda i: (0, 0)
    return pl.pallas_call(
        _ple_kernel,
        grid=(n // tm,),
        in_specs=[pl.BlockSpec((tm, d), lambda i: (i, 0)),
                  pl.BlockSpec((tm, pd), lambda i: (i, 0)),
                  pl.BlockSpec((d, d), const),
                  pl.BlockSpec((1, d), const),
                  pl.BlockSpec((pd, d), const),
                  pl.BlockSpec((1, d), const),
                  pl.BlockSpec((1, d), const)],
        out_specs=pl.BlockSpec((tm, d), lambda i: (i, 0)),
        out_shape=jax.ShapeDtypeStruct((n, d), F32),
        compiler_params=_params("parallel"),
        name="ple_ln",
    )(x, p, w_gate, b_gate, w_proj, g, b)


def _swap_halves_cols(w):
    half = w.shape[-1] // 2
    return jnp.concatenate([w[..., half:], w[..., :half]], -1)


def _tile(n, pref):
    t = min(n, pref)
    assert n % t == 0
    return t


def _layer(x, p, rope_tab, prm, *, batch):
    n, d = x.shape
    row = lambda v: v.reshape(1, -1)
    tm = _tile(n // batch, 512)
    tm_ffn = _tile(n, 1024)

    tf = _tile(prm["ffn1_w_gate"].shape[1], 512)
    half = FFN_CAST_SCALES[2]
    x, ffn2_w = ffn_ln(x, prm["ffn1_w_gate"].astype(BF16), prm["ffn1_w_up"].astype(BF16),
                       (half * prm["ffn1_w_down"]).astype(BF16), row(prm["ln1_g"]), row(prm["ln1_b"]),
                       tm=tm_ffn, tf=tf,
                       cast=(prm["ffn2_w_gate"], prm["ffn2_w_up"], prm["ffn2_w_down"]))

    lru_w = prm["conv_w"].shape[1]
    q_rank = prm["w_q_up"].shape[0]
    wq = prm["w_q_up"].astype(BF16).reshape(q_rank, MLA_HEADS, QK_NOPE_DIM + QK_ROPE_DIM)
    wq_ext = jnp.concatenate([wq, _swap_halves_cols(wq[..., QK_NOPE_DIM:])], -1)
    wq_ext = wq_ext.reshape(q_rank, -1)
    lru_in, gate, q, k, vt = in_proj(
        x, prm["w_in"].astype(BF16), wq_ext, prm["w_kv_up"].astype(BF16), row(prm["q_norm_g"]),
        row(prm["kv_norm_g"]), rope_tab, batch=batch, tm=tm, lru_w=lru_w)

    s = n // batch
    y_lru = rglru(lru_in.reshape(batch, s, lru_w), gate.reshape(batch, s, lru_w),
                  prm["conv_w"], row(prm["conv_b"]),
                  _block_diag_groups(prm["lru_w_a"]).astype(BF16), row(prm["lru_b_a"]),
                  _block_diag_groups(prm["lru_w_x"]).astype(BF16), row(prm["lru_b_x"]),
                  row(prm["lru_lambda"]), ts=_tile(s, 512))
    y_mla = attention(q, k, vt)

    x = out_proj_ln(x, y_lru.reshape(n, lru_w), y_mla.reshape(n, -1), prm["w_out"].astype(BF16),
                    row(prm["ln2_g"]), row(prm["ln2_b"]), tm=tm)

    x, _ = ffn_ln(x, *ffn2_w, row(prm["ln3_g"]), row(prm["ln3_b"]), tm=tm_ffn, tf=tf)

    return ple_ln(x, p, prm["ple_w_gate"].astype(BF16), row(prm["ple_b_gate"]),
                  prm["ple_w_proj"].astype(BF16), row(prm["ln4_g"]), row(prm["ln4_b"]), tm=tm)


def kernel(x, p, positions, ffn1_w_gate, ffn1_w_up, ffn1_w_down, ln1_g, ln1_b, w_in, conv_w, conv_b, lru_w_a, lru_b_a, lru_w_x, lru_b_x, lru_lambda, q_norm_g, w_q_up, kv_norm_g, w_kv_up, w_out, ln2_g, ln2_b, ffn2_w_gate, ffn2_w_up, ffn2_w_down, ln3_g, ln3_b, ple_w_gate, ple_b_gate, ple_w_proj, ln4_g, ln4_b):
    names = ("ffn1_w_gate ffn1_w_up ffn1_w_down ln1_g ln1_b w_in conv_w conv_b lru_w_a lru_b_a lru_w_x "
             "lru_b_x lru_lambda q_norm_g w_q_up kv_norm_g w_kv_up w_out ln2_g ln2_b ffn2_w_gate ffn2_w_up "
             "ffn2_w_down ln3_g ln3_b ple_w_gate ple_b_gate ple_w_proj ln4_g ln4_b").split()
    stacked = dict(zip(names, (ffn1_w_gate, ffn1_w_up, ffn1_w_down, ln1_g, ln1_b, w_in, conv_w, conv_b,
                               lru_w_a, lru_b_a, lru_w_x, lru_b_x, lru_lambda, q_norm_g, w_q_up, kv_norm_g,
                               w_kv_up, w_out, ln2_g, ln2_b, ffn2_w_gate, ffn2_w_up, ffn2_w_down, ln3_g,
                               ln3_b, ple_w_gate, ple_b_gate, ple_w_proj, ln4_g, ln4_b)))
    batch, seq, d = x.shape
    rope_tab = rope_table(positions, tn=_tile(batch * seq, 4096))
    h = x.reshape(batch * seq, d)
    assert ffn1_w_gate.shape[0] == DEPTH
    for i in range(DEPTH):
        prm = {kname: val[i] for kname, val in stacked.items()}
        h = _layer(h, p[i].reshape(batch * seq, -1), rope_tab, prm, batch=batch)
    return h.reshape(batch, seq, d)
```

```python
import functools
import math

import jax
import jax.numpy as jnp
import numpy as np
from jax import lax
from jax.experimental import pallas as pl
from jax.experimental.pallas import tpu as pltpu

LRU_BLOCKS = 16
CONV_WIDTH = 4
LRU_C = 8.0
MLA_HEADS = 8
QK_NOPE_DIM = 128
QK_ROPE_DIM = 64
V_HEAD_DIM = 128
ROPE_THETA = 10000.0
LN_EPS = 1e-5
RMS_EPS = 1e-6
DEPTH = 1
DEEPNORM_ALPHA = (2 * DEPTH) ** 0.25

LANES = 128
SUBLANES = 8
MXU_DIM = 256
VMEM_LIMIT = 61 * 1024 * 1024
LOG2E = math.log2(math.e)
LN_ROWS = 256
MM_ROWS = 256
SCAN_ROWS = 32
FFN_CAST_SCALES = (1.0, 1.0, 0.5)

BF16 = jnp.bfloat16
F32 = jnp.float32


def _params(*sem):
    return pltpu.CompilerParams(dimension_semantics=sem, vmem_limit_bytes=VMEM_LIMIT)


def _layer_norm(y, g, b):
    mu = jnp.mean(y, axis=-1, keepdims=True)
    yc = y - mu
    var = jnp.mean(yc * yc, axis=-1, keepdims=True)
    return yc * lax.rsqrt(var + LN_EPS) * g + b


def _rms_norm(y, g):
    return y * lax.rsqrt(jnp.mean(y * y, axis=-1, keepdims=True) + RMS_EPS) * g


def _dot(a, b):
    return jnp.dot(a, b, preferred_element_type=F32)


def _sigmoid(x):
    return 0.5 * jnp.tanh(0.5 * x) + 0.5


def _rope_kernel(inv_ref, pos_ref, tab_ref):
    half = inv_ref.shape[0]
    ang = pos_ref[...].astype(F32) * inv_ref[...]
    cos = jnp.cos(ang)
    sin = jnp.sin(ang)
    tab_ref[0 * half:1 * half, :] = cos
    tab_ref[1 * half:2 * half, :] = cos
    tab_ref[2 * half:3 * half, :] = -sin
    tab_ref[3 * half:4 * half, :] = sin


def rope_table(positions, *, tn):
    n = positions.size
    half = QK_ROPE_DIM // 2
    inv_freq = ROPE_THETA ** (-jnp.arange(0, QK_ROPE_DIM, 2, dtype=F32) / QK_ROPE_DIM)
    return pl.pallas_call(
        _rope_kernel,
        grid=(n // tn,),
        in_specs=[pl.BlockSpec((half, 1), lambda i: (0, 0)),
                  pl.BlockSpec((1, tn), lambda i: (0, i))],
        out_specs=pl.BlockSpec((4 * half, tn), lambda i: (0, i)),
        out_shape=jax.ShapeDtypeStruct((4 * half, n), F32),
        compiler_params=_params("parallel"),
        name="rope_table",
    )(inv_freq.reshape(half, 1), positions.reshape(1, n))


def _ffn_ln_kernel(x_ref, wg_ref, wu_ref, wd_ref, g_ref, b_ref, *rest, ncast):
    cast_in, o_ref, cast_out, xb_ref = rest[:ncast], rest[ncast], rest[ncast + 1:-1], rest[-1]
    j = pl.program_id(1)

    @pl.when(j == 0)
    def _():
        xb_ref[...] = x_ref[...].astype(BF16)
        o_ref[...] = DEEPNORM_ALPHA * x_ref[...]

    xb = xb_ref[...]
    gate = _dot(xb, wg_ref[...])
    up = _dot(xb, wu_ref[...])
    h = (gate * _sigmoid(gate) * up).astype(BF16)
    o_ref[...] += _dot(h, wd_ref[...])

    for idx, (src, dst) in enumerate(zip(cast_in, cast_out)):
        dst[...] = (src[...] * FFN_CAST_SCALES[idx]).astype(BF16)

    @pl.when(j == pl.num_programs(1) - 1)
    def _():
        def rows(c, _):
            r = pl.ds(pl.multiple_of(c * LN_ROWS, LN_ROWS), LN_ROWS)
            o_ref[r, :] = _layer_norm(o_ref[r, :], g_ref[...], b_ref[...])
            return 0

        lax.fori_loop(0, o_ref.shape[0] // LN_ROWS, rows, 0)


def ffn_ln(x, wg, wu, wd_half, g, b, *, tm, tf, cast=()):
    n, d = x.shape
    f = wg.shape[1]
    ni, nj = n // tm, f // tf
    assert len(cast) in (0, len(FFN_CAST_SCALES))
    up_spec = pl.BlockSpec((d // ni, tf), lambda i, j: (i, j))
    down_spec = pl.BlockSpec((f // (ni * nj), d), lambda i, j: (i * nj + j, 0))
    cast_specs = [up_spec, up_spec, down_spec][:len(cast)]
    out = pl.pallas_call(
        functools.partial(_ffn_ln_kernel, ncast=len(cast)),
        grid=(ni, nj),
        in_specs=[pl.BlockSpec((tm, d), lambda i, j: (i, 0)),
                  pl.BlockSpec((d, tf), lambda i, j: (0, j)),
                  pl.BlockSpec((d, tf), lambda i, j: (0, j)),
                  pl.BlockSpec((tf, d), lambda i, j: (j, 0)),
                  pl.BlockSpec((1, d), lambda i, j: (0, 0)),
                  pl.BlockSpec((1, d), lambda i, j: (0, 0))] + cast_specs,
        out_specs=[pl.BlockSpec((tm, d), lambda i, j: (i, 0))] + cast_specs,
        out_shape=[jax.ShapeDtypeStruct((n, d), F32)] + [jax.ShapeDtypeStruct(w.shape, BF16) for w in cast],
        scratch_shapes=[pltpu.VMEM((tm, d), BF16)],
        compiler_params=_params("parallel", "arbitrary"),
        name="ffn_ln",
    )(x, wg, wu, wd_half, g, b, *cast)
    return out[0], tuple(out[1:])


def _in_proj_kernel(x_ref, win_ref, wq_ref, wkv_ref, qg_ref, kvg_ref, rope_ref,
                    lru_ref, gate_ref, q_ref, k_ref, v_ref, *, lru_w, q_rank, kv_rank):
    xb = x_ref[...].astype(BF16)
    nope, rope, vd = QK_NOPE_DIM, QK_ROPE_DIM, V_HEAD_DIM
    tab = rope_ref[...].T

    def rotate(pe_and_swapped):
        prod = pe_and_swapped * tab
        return prod[:, :rope] + prod[:, rope:]

    lru_ref[...] = _dot(xb, win_ref[:, 0:lru_w])
    gate_ref[...] = jax.nn.gelu(_dot(xb, win_ref[:, lru_w:2 * lru_w])).astype(BF16)

    o = 2 * lru_w
    c_q = _dot(xb, win_ref[:, o:o + q_rank])
    o += q_rank
    c_kv = _dot(xb, win_ref[:, o:o + kv_rank])
    o += kv_rank
    kpe = _dot(xb, win_ref[:, o:o + rope])
    k_rot = rotate(jnp.concatenate([kpe, kpe[:, rope // 2:], kpe[:, :rope // 2]], -1))

    qn = _rms_norm(c_q, qg_ref[...]).astype(BF16)
    kvn = _rms_norm(c_kv, kvg_ref[...]).astype(BF16)
    qscale = (nope + rope) ** -0.5 * LOG2E
    hq = nope + 2 * rope
    for h in range(MLA_HEADS):
        qh = _dot(qn, wq_ref[:, h * hq:(h + 1) * hq])
        qrot = jnp.concatenate([qh[:, :nope], rotate(qh[:, nope:])], -1) * qscale
        q_ref[0, h, 0] = qrot.T.astype(BF16)
        kvh = _dot(kvn, wkv_ref[:, h * (nope + vd):(h + 1) * (nope + vd)])
        k_ref[0, h] = jnp.concatenate([kvh[:, :nope], k_rot], -1).astype(BF16)
        vt = kvh[:, nope:].T.astype(BF16)
        tk = v_ref.shape[-1]
        for c in range(v_ref.shape[2]):
            v_ref[0, h, c, 0:vd] = vt[:, c * tk:(c + 1) * tk]
            v_ref[0, h, c, vd:] = jnp.ones((v_ref.shape[3] - vd, tk), BF16)


def in_proj(x, w_in, w_q, w_kv, q_g, kv_g, rope_tab, *, batch, tm, lru_w):
    n, d = x.shape
    s = n // batch
    spb = s // tm
    q_rank, kv_rank = w_q.shape[0], w_kv.shape[0]
    qk = QK_NOPE_DIM + QK_ROPE_DIM
    tk = min(ATTN_TK, tm)
    const = lambda i: (0, 0)
    head_map = lambda i: (i // spb, 0, i % spb, 0)
    kern = functools.partial(_in_proj_kernel, lru_w=lru_w, q_rank=q_rank, kv_rank=kv_rank)
    return pl.pallas_call(
        kern,
        grid=(n // tm,),
        in_specs=[pl.BlockSpec((tm, d), lambda i: (i, 0)),
                  pl.BlockSpec(w_in.shape, const),
                  pl.BlockSpec(w_q.shape, const),
                  pl.BlockSpec(w_kv.shape, const),
                  pl.BlockSpec((1, q_rank), const),
                  pl.BlockSpec((1, kv_rank), const),
                  pl.BlockSpec((2 * QK_ROPE_DIM, tm), lambda i: (0, i))],
        out_specs=[pl.BlockSpec((tm, lru_w), lambda i: (i, 0)),
                   pl.BlockSpec((tm, lru_w), lambda i: (i, 0)),
                   pl.BlockSpec((1, MLA_HEADS, 1, qk, tm), lambda i: (i // spb, 0, i % spb, 0, 0)),
                   pl.BlockSpec((1, MLA_HEADS, tm, qk), head_map),
                   pl.BlockSpec((1, MLA_HEADS, tm // tk, V_HEAD_DIM + ATTN_ONES_ROWS, tk),
                                lambda i: (i // spb, 0, i % spb, 0, 0))],
        out_shape=[jax.ShapeDtypeStruct((n, lru_w), F32),
                   jax.ShapeDtypeStruct((n, lru_w), BF16),
                   jax.ShapeDtypeStruct((batch, MLA_HEADS, spb, qk, tm), BF16),
                   jax.ShapeDtypeStruct((batch, MLA_HEADS, s, qk), BF16),
                   jax.ShapeDtypeStruct((batch, MLA_HEADS, s // tk, V_HEAD_DIM + ATTN_ONES_ROWS, tk), BF16)],
        compiler_params=_params("parallel"),
        name="in_proj",
    )(x, w_in, w_q, w_kv, q_g, kv_g, rope_tab)


def _rglru_kernel(x_ref, gate_ref, cw_ref, cb_ref, wa_ref, ba_ref, wx_ref, bx_ref, lam_ref,
                  y_ref, xbuf, a_scr, u_scr, hout_scr, h_scr, *, ts, pad):
    t = pl.program_id(1)
    w = x_ref.shape[-1]

    @pl.when(t == 0)
    def _():
        xbuf[0:pad, :] = jnp.zeros((pad, w), F32)
        h_scr[0:1, :] = jnp.zeros((1, w), F32)

    xbuf[pad:pad + ts, :] = x_ref[0]
    xc = cb_ref[...] + cw_ref[CONV_WIDTH - 1:CONV_WIDTH, :] * xbuf[pad:pad + ts, :]
    for k in range(CONV_WIDTH - 1):
        sh = CONV_WIDTH - 1 - k
        xc = xc + cw_ref[k:k + 1, :] * xbuf[pad - sh:pad - sh + ts, :]
    xbuf[0:pad, :] = xbuf[ts:ts + pad, :]

    xcb = xc.astype(BF16)
    ngrp = w // MXU_DIM
    rec = jnp.concatenate(
        [_dot(xcb[:, g * MXU_DIM:(g + 1) * MXU_DIM], wa_ref[g]) for g in range(ngrp)], -1)
    ing = jnp.concatenate(
        [_dot(xcb[:, g * MXU_DIM:(g + 1) * MXU_DIM], wx_ref[g]) for g in range(ngrp)], -1)
    rec = _sigmoid(rec + ba_ref[...])
    ing = _sigmoid(ing + bx_ref[...])

    nlam = -lam_ref[...]
    softplus = jnp.maximum(nlam, 0.0) + jnp.log1p(jnp.exp(-jnp.abs(nlam)))
    log_a = -LRU_C * rec * softplus
    a = jnp.exp(log_a)
    mult = jnp.sqrt(jnp.tanh(-log_a) * (1.0 + a * a))
    a_scr[...] = a.reshape(a_scr.shape)
    u_scr[...] = (mult * (ing * xc)).reshape(u_scr.shape)

    def group(g, h):
        for k in range(0, SCAN_ROWS, 2):
            t0, t1 = g * (SCAN_ROWS // SUBLANES) + k // SUBLANES, g * (SCAN_ROWS // SUBLANES) + (k + 1) // SUBLANES
            r0, r1 = pl.ds(k % SUBLANES, 1), pl.ds((k + 1) % SUBLANES, 1)
            a0, u0, a1, u1 = a_scr[t0, r0, :], u_scr[t0, r0, :], a_scr[t1, r1, :], u_scr[t1, r1, :]
            hout_scr[t0, r0, :] = a0 * h + u0
            h = (a1 * a0) * h + (a1 * u0 + u1)
            hout_scr[t1, r1, :] = h
        return h

    h_last = lax.fori_loop(0, ts // SCAN_ROWS, group, h_scr[0:1, :])
    h_scr[0:1, :] = h_last
    y_ref[0] = (gate_ref[0].astype(F32) * hout_scr[...].reshape(ts, w)).astype(BF16)


def rglru(lru_in, gate, conv_w, conv_b, wa_bd, b_a, wx_bd, b_x, lam, *, ts):
    bsz, s, w = lru_in.shape
    pad = 8
    const2 = lambda b, t: (0, 0)
    const3 = lambda b, t: (0, 0, 0)
    kern = functools.partial(_rglru_kernel, ts=ts, pad=pad)
    return pl.pallas_call(
        kern,
        grid=(bsz, s // ts),
        in_specs=[pl.BlockSpec((1, ts, w), lambda b, t: (b, t, 0)),
                  pl.BlockSpec((1, ts, w), lambda b, t: (b, t, 0)),
                  pl.BlockSpec(conv_w.shape, const2),
                  pl.BlockSpec((1, w), const2),
                  pl.BlockSpec(wa_bd.shape, const3),
                  pl.BlockSpec((1, w), const2),
                  pl.BlockSpec(wx_bd.shape, const3),
                  pl.BlockSpec((1, w), const2),
                  pl.BlockSpec((1, w), const2)],
        out_specs=pl.BlockSpec((1, ts, w), lambda b, t: (b, t, 0)),
        out_shape=jax.ShapeDtypeStruct((bsz, s, w), BF16),
        scratch_shapes=[pltpu.VMEM((ts + pad, w), F32),
                        pltpu.VMEM((ts // SUBLANES, SUBLANES, w), F32),
                        pltpu.VMEM((ts // SUBLANES, SUBLANES, w), F32),
                        pltpu.VMEM((ts // SUBLANES, SUBLANES, w), F32),
                        pltpu.VMEM((8, w), F32)],
        compiler_params=_params("parallel", "arbitrary"),
        name="rglru",
    )(lru_in, gate, conv_w, conv_b, wa_bd, b_a, wx_bd, b_x, lam)


def _block_diag_groups(wblk):
    g, d, _ = wblk.shape
    per = MXU_DIM // d
    wg = wblk.reshape(g // per, per, d, d)
    eye = jnp.eye(per, dtype=wblk.dtype)
    out = jnp.einsum('gpij,pq->gpiqj', wg, eye)
    return out.reshape(g // per, MXU_DIM, MXU_DIM)


HEADS_PER_STEP = 2
ATTN_TK = 256
ATTN_ONES_ROWS = 16
ATTN_KV_UNROLL = 2


def _attn_kernel(q_ref, k_ref, vt_ref, o_ref, *scratch, tq):
    nh, nq = q_ref.shape[0], q_ref.shape[1]
    dva, tk = vt_ref.shape[2], vt_ref.shape[3]
    dv = dva - ATTN_ONES_ROWS
    sub = tq // tk
    assert sub % 2 == 0
    heads = range(nh)
    st_scr = [scratch[2 * hh:2 * hh + 2] for hh in heads]
    acc_scr = scratch[2 * nh:]

    def scores(qs, kblk, slot):
        for hh in heads:
            st_scr[hh][slot][...] = _dot(k_ref[hh, pl.ds(kblk * tk, tk), :], qs[hh])

    def update(hh, kblk, slot, diag_off, m):
        st = st_scr[hh][slot][...]
        if diag_off is not None:
            key = lax.broadcasted_iota(jnp.int32, (tk, tq), 0) + diag_off
            qry = lax.broadcasted_iota(jnp.int32, (tk, tq), 1)
            st = jnp.where(key <= qry, st, -jnp.inf)
        m_new = jnp.maximum(m, jnp.max(st, axis=0, keepdims=True))
        pt = jnp.exp2(st - m_new).astype(BF16)
        acc_scr[hh][...] = (jnp.exp2(m - m_new) * acc_scr[hh][...]
                            + _dot(vt_ref[hh, kblk], pt))
        return m_new

    def block(kblk, slot, diag_off, ms, ahead):
        scores(*ahead, 1 - slot)
        return tuple(update(hh, kblk, slot, diag_off, ms[hh]) for hh in heads)

    def load_q(qi):
        return [q_ref[hh, qi] for hh in heads]

    scores(load_q(0), 0, 0)

    def q_tile(qi, _):
        qs = load_q(qi)
        for hh in heads:
            acc_scr[hh][...] = jnp.zeros((dva, tq), F32)

        def kv_tiles(ntiles, kj, ms):
            for c in range(ntiles * sub):
                ms = block(kj * sub + c, c % 2, None, ms, (qs, kj * sub + c + 1))
            return ms

        n_main = qi // ATTN_KV_UNROLL
        ms = tuple(jnp.full((1, tq), -jnp.inf, F32) for _ in heads)
        ms = lax.fori_loop(0, n_main,
                           lambda it, ms: kv_tiles(ATTN_KV_UNROLL, it * ATTN_KV_UNROLL, ms), ms)
        ms = lax.fori_loop(n_main * ATTN_KV_UNROLL, qi, functools.partial(kv_tiles, 1), ms)
        q_next = load_q(jnp.minimum(qi + 1, nq - 1))
        for c in range(sub):
            ahead = (qs, qi * sub + c + 1) if c + 1 < sub else (q_next, 0)
            ms = block(qi * sub + c, c % 2, c * tk, ms, ahead)
        for hh in heads:
            out = acc_scr[hh][0:dv, :] / acc_scr[hh][dv:dv + 1, :]
            o_ref[pl.ds(qi * tq, tq), hh * dv:(hh + 1) * dv] = out.T.astype(o_ref.dtype)
        return 0

    lax.fori_loop(0, nq, q_tile, 0)


def attention(qt, k, vt):
    bsz, nh, nq, dqk, tq = qt.shape
    s = nq * tq
    _, _, nkb, dva, tk = vt.shape
    dv = dva - ATTN_ONES_ROWS
    hps = HEADS_PER_STEP
    return pl.pallas_call(
        functools.partial(_attn_kernel, tq=tq),
        grid=(bsz, nh // hps),
        in_specs=[pl.BlockSpec((None, hps, nq, dqk, tq), lambda b, h: (b, h, 0, 0, 0)),
                  pl.BlockSpec((None, hps, s, dqk), lambda b, h: (b, h, 0, 0)),
                  pl.BlockSpec((None, hps, nkb, dva, tk), lambda b, h: (b, h, 0, 0, 0))],
        out_specs=pl.BlockSpec((None, s, hps * dv), lambda b, h: (b, 0, h)),
        out_shape=jax.ShapeDtypeStruct((bsz, s, nh * dv), BF16),
        scratch_shapes=([pltpu.VMEM((tk, tq), F32)] * (2 * hps) + [pltpu.VMEM((dva, tq), F32)] * hps),
        compiler_params=_params("parallel", "parallel"),
        name="mla_attention",
    )(qt, k, vt)


def _row_chunks(nrows):
    step = min(MM_ROWS, nrows)
    return [slice(r, r + step) for r in range(0, nrows, step)]


def _out_proj_kernel(x_ref, yl_ref, ym_ref, w_ref, g_ref, b_ref, o_ref):
    wl = yl_ref.shape[1]
    for r in _row_chunks(x_ref.shape[0]):
        mix = _dot(yl_ref[r, :], w_ref[0:wl, :]) + _dot(ym_ref[r, :], w_ref[wl:, :])
        o_ref[r, :] = _layer_norm(DEEPNORM_ALPHA * x_ref[r, :] + mix, g_ref[...], b_ref[...])


def out_proj_ln(x, y_lru, y_mla, w_out, g, b, *, tm):
    n, d = x.shape
    wl, wm = y_lru.shape[1], y_mla.shape[1]
    const = lambda i: (0, 0)
    return pl.pallas_call(
        _out_proj_kernel,
        grid=(n // tm,),
        in_specs=[pl.BlockSpec((tm, d), lambda i: (i, 0)),
                  pl.BlockSpec((tm, wl), lambda i: (i, 0)),
                  pl.BlockSpec((tm, wm), lambda i: (i, 0)),
                  pl.BlockSpec((wl + wm, d), const),
                  pl.BlockSpec((1, d), const),
                  pl.BlockSpec((1, d), const)],
        out_specs=pl.BlockSpec((tm, d), lambda i: (i, 0)),
        out_shape=jax.ShapeDtypeStruct((n, d), F32),
        compiler_params=_params("parallel"),
        name="out_proj_ln",
    )(x, y_lru, y_mla, w_out, g, b)


def _ple_kernel(x_ref, p_ref, wg_ref, bg_ref, wp_ref, g_ref, b_ref, o_ref):
    for r in _row_chunks(x_ref.shape[0]):
        x = x_ref[r, :]
        gate = _sigmoid(_dot(x.astype(BF16), wg_ref[...]) + bg_ref[...])
        proj = _dot(p_ref[r, :].astype(BF16), wp_ref[...])
        o_ref[r, :] = _layer_norm(DEEPNORM_ALPHA * x + gate * proj, g_ref[...], b_ref[...])


def ple_ln(x, p, w_gate, b_gate, w_proj, g, b, *, tm):
    n, d = x.shape
    pd = p.shape[1]
    const = lambda i: (0, 0)
    return pl.pallas_call(
        _ple_kernel,
        grid=(n // tm,),
        in_specs=[pl.BlockSpec((tm, d), lambda i: (i, 0)),
                  pl.BlockSpec((tm, pd), lambda i: (i, 0)),
                  pl.BlockSpec((d, d), const),
                  pl.BlockSpec((1, d), const),
                  pl.BlockSpec((pd, d), const),
                  pl.BlockSpec((1, d), const),
                  pl.BlockSpec((1, d), const)],
        out_specs=pl.BlockSpec((tm, d), lambda i: (i, 0)),
        out_shape=jax.ShapeDtypeStruct((n, d), F32),
        compiler_params=_params("parallel"),
        name="ple_ln",
    )(x, p, w_gate, b_gate, w_proj, g, b)


def _swap_halves_cols(w):
    half = w.shape[-1] // 2
    return jnp.concatenate([w[..., half:], w[..., :half]], -1)


def _tile(n, pref):
    t = min(n, pref)
    assert n % t == 0
    return t


def _layer(x, p, rope_tab, prm, *, batch):
    n, d = x.shape
    row = lambda v: v.reshape(1, -1)
    tm = _tile(n // batch, 512)
    tm_ffn = _tile(n, 1024)

    tf = _tile(prm["ffn1_w_gate"].shape[1], 512)
    half = FFN_CAST_SCALES[2]
    x, ffn2_w = ffn_ln(x, prm["ffn1_w_gate"].astype(BF16), prm["ffn1_w_up"].astype(BF16),
                       (half * prm["ffn1_w_down"]).astype(BF16), row(prm["ln1_g"]), row(prm["ln1_b"]),
                       tm=tm_ffn, tf=tf,
                       cast=(prm["ffn2_w_gate"], prm["ffn2_w_up"], prm["ffn2_w_down"]))

    lru_w = prm["conv_w"].shape[1]
    q_rank = prm["w_q_up"].shape[0]
    wq = prm["w_q_up"].astype(BF16).reshape(q_rank, MLA_HEADS, QK_NOPE_DIM + QK_ROPE_DIM)
    wq_ext = jnp.concatenate([wq, _swap_halves_cols(wq[..., QK_NOPE_DIM:])], -1)
    wq_ext = wq_ext.reshape(q_rank, -1)
    lru_in, gate, q, k, vt = in_proj(
        x, prm["w_in"].astype(BF16), wq_ext, prm["w_kv_up"].astype(BF16), row(prm["q_norm_g"]),
        row(prm["kv_norm_g"]), rope_tab, batch=batch, tm=tm, lru_w=lru_w)

    s = n // batch
    y_lru = rglru(lru_in.reshape(batch, s, lru_w), gate.reshape(batch, s, lru_w),
                  prm["conv_w"], row(prm["conv_b"]),
                  _block_diag_groups(prm["lru_w_a"]).astype(BF16), row(prm["lru_b_a"]),
                  _block_diag_groups(prm["lru_w_x"]).astype(BF16), row(prm["lru_b_x"]),
                  row(prm["lru_lambda"]), ts=_tile(s, 512))
    y_mla = attention(q, k, vt)

    x = out_proj_ln(x, y_lru.reshape(n, lru_w), y_mla.reshape(n, -1), prm["w_out"].astype(BF16),
                    row(prm["ln2_g"]), row(prm["ln2_b"]), tm=tm)

    x, _ = ffn_ln(x, *ffn2_w, row(prm["ln3_g"]), row(prm["ln3_b"]), tm=tm_ffn, tf=tf)

    return ple_ln(x, p, prm["ple_w_gate"].astype(BF16), row(prm["ple_b_gate"]),
                  prm["ple_w_proj"].astype(BF16), row(prm["ln4_g"]), row(prm["ln4_b"]), tm=tm)


def kernel(x, p, positions, ffn1_w_gate, ffn1_w_up, ffn1_w_down, ln1_g, ln1_b, w_in, conv_w, conv_b, lru_w_a, lru_b_a, lru_w_x, lru_b_x, lru_lambda, q_norm_g, w_q_up, kv_norm_g, w_kv_up, w_out, ln2_g, ln2_b, ffn2_w_gate, ffn2_w_up, ffn2_w_down, ln3_g, ln3_b, ple_w_gate, ple_b_gate, ple_w_proj, ln4_g, ln4_b):
    names = ("ffn1_w_gate ffn1_w_up ffn1_w_down ln1_g ln1_b w_in conv_w conv_b lru_w_a lru_b_a lru_w_x "
             "lru_b_x lru_lambda q_norm_g w_q_up kv_norm_g w_kv_up w_out ln2_g ln2_b ffn2_w_gate ffn2_w_up "
             "ffn2_w_down ln3_g ln3_b ple_w_gate ple_b_gate ple_w_proj ln4_g ln4_b").split()
    stacked = dict(zip(names, (ffn1_w_gate, ffn1_w_up, ffn1_w_down, ln1_g, ln1_b, w_in, conv_w, conv_b,
                               lru_w_a, lru_b_a, lru_w_x, lru_b_x, lru_lambda, q_norm_g, w_q_up, kv_norm_g,
                               w_kv_up, w_out, ln2_g, ln2_b, ffn2_w_gate, ffn2_w_up, ffn2_w_down, ln3_g,
                               ln3_b, ple_w_gate, ple_b_gate, ple_w_proj, ln4_g, ln4_b)))
    batch, seq, d = x.shape
    rope_tab = rope_table(positions, tn=_tile(batch * seq, 4096))
    h = x.reshape(batch * seq, d)
    assert ffn1_w_gate.shape[0] == DEPTH
    for i in range(DEPTH):
        prm = {kname: val[i] for kname, val in stacked.items()}
        h = _layer(h, p[i].reshape(batch * seq, -1), rope_tab, prm, batch=batch)
    return h.reshape(batch, seq, d)
```

```python
import functools
import math

import jax
import jax.numpy as jnp
import numpy as np
from jax import lax
from jax.experimental import pallas as pl
from jax.experimental.pallas import tpu as pltpu

LRU_BLOCKS = 16
CONV_WIDTH = 4
LRU_C = 8.0
MLA_HEADS = 8
QK_NOPE_DIM = 128
QK_ROPE_DIM = 64
V_HEAD_DIM = 128
ROPE_THETA = 10000.0
LN_EPS = 1e-5
RMS_EPS = 1e-6
DEPTH = 1
DEEPNORM_ALPHA = (2 * DEPTH) ** 0.25

LANES = 128
SUBLANES = 8
MXU_DIM = 256
VMEM_LIMIT = 61 * 1024 * 1024
LOG2E = math.log2(math.e)
LN_ROWS = 256
MM_ROWS = 256
SCAN_ROWS = 32
FFN_CAST_SCALES = (1.0, 1.0, 0.5)

BF16 = jnp.bfloat16
F32 = jnp.float32


def _params(*sem):
    return pltpu.CompilerParams(dimension_semantics=sem, vmem_limit_bytes=VMEM_LIMIT)


def _layer_norm(y, g, b):
    mu = jnp.mean(y, axis=-1, keepdims=True)
    yc = y - mu
    var = jnp.mean(yc * yc, axis=-1, keepdims=True)
    return yc * lax.rsqrt(var + LN_EPS) * g + b


def _rms_norm(y, g):
    return y * lax.rsqrt(jnp.mean(y * y, axis=-1, keepdims=True) + RMS_EPS) * g


def _dot(a, b):
    return jnp.dot(a, b, preferred_element_type=F32)


def _sigmoid(x):
    return 0.5 * jnp.tanh(0.5 * x) + 0.5


def _rope_kernel(inv_ref, pos_ref, tab_ref):
    half = inv_ref.shape[0]
    ang = pos_ref[...].astype(F32) * inv_ref[...]
    cos = jnp.cos(ang)
    sin = jnp.sin(ang)
    tab_ref[0 * half:1 * half, :] = cos
    tab_ref[1 * half:2 * half, :] = cos
    tab_ref[2 * half:3 * half, :] = -sin
    tab_ref[3 * half:4 * half, :] = sin


def rope_table(positions, *, tn):
    n = positions.size
    half = QK_ROPE_DIM // 2
    inv_freq = ROPE_THETA ** (-jnp.arange(0, QK_ROPE_DIM, 2, dtype=F32) / QK_ROPE_DIM)
    return pl.pallas_call(
        _rope_kernel,
        grid=(n // tn,),
        in_specs=[pl.BlockSpec((half, 1), lambda i: (0, 0)),
                  pl.BlockSpec((1, tn), lambda i: (0, i))],
        out_specs=pl.BlockSpec((4 * half, tn), lambda i: (0, i)),
        out_shape=jax.ShapeDtypeStruct((4 * half, n), F32),
        compiler_params=_params("parallel"),
        name="rope_table",
    )(inv_freq.reshape(half, 1), positions.reshape(1, n))


def _ffn_ln_kernel(x_ref, wg_ref, wu_ref, wd_ref, g_ref, b_ref, *rest, ncast):
    cast_in, o_ref, cast_out, xb_ref = rest[:ncast], rest[ncast], rest[ncast + 1:-1], rest[-1]
    j = pl.program_id(1)

    @pl.when(j == 0)
    def _():
        xb_ref[...] = x_ref[...].astype(BF16)
        o_ref[...] = DEEPNORM_ALPHA * x_ref[...]

    xb = xb_ref[...]
    gate = _dot(xb, wg_ref[...])
    up = _dot(xb, wu_ref[...])
    h = (gate * _sigmoid(gate) * up).astype(BF16)
    o_ref[...] += _dot(h, wd_ref[...])

    for idx, (src, dst) in enumerate(zip(cast_in, cast_out)):
        dst[...] = (src[...] * FFN_CAST_SCALES[idx]).astype(BF16)

    @pl.when(j == pl.num_programs(1) - 1)
    def _():
        def rows(c, _):
            r = pl.ds(pl.multiple_of(c * LN_ROWS, LN_ROWS), LN_ROWS)
            o_ref[r, :] = _layer_norm(o_ref[r, :], g_ref[...], b_ref[...])
            return 0

        lax.fori_loop(0, o_ref.shape[0] // LN_ROWS, rows, 0)


def ffn_ln(x, wg, wu, wd_half, g, b, *, tm, tf, cast=()):
    n, d = x.shape
    f = wg.shape[1]
    ni, nj = n // tm, f // tf
    assert len(cast) in (0, len(FFN_CAST_SCALES))
    up_spec = pl.BlockSpec((d // ni, tf), lambda i, j: (i, j))
    down_spec = pl.BlockSpec((f // (ni * nj), d), lambda i, j: (i * nj + j, 0))
    cast_specs = [up_spec, up_spec, down_spec][:len(cast)]
    out = pl.pallas_call(
        functools.partial(_ffn_ln_kernel, ncast=len(cast)),
        grid=(ni, nj),
        in_specs=[pl.BlockSpec((tm, d), lambda i, j: (i, 0)),
                  pl.BlockSpec((d, tf), lambda i, j: (0, j)),
                  pl.BlockSpec((d, tf), lambda i, j: (0, j)),
                  pl.BlockSpec((tf, d), lambda i, j: (j, 0)),
                  pl.BlockSpec((1, d), lambda i, j: (0, 0)),
                  pl.BlockSpec((1, d), lambda i, j: (0, 0))] + cast_specs,
        out_specs=[pl.BlockSpec((tm, d), lambda i, j: (i, 0))] + cast_specs,
        out_shape=[jax.ShapeDtypeStruct((n, d), F32)] + [jax.ShapeDtypeStruct(w.shape, BF16) for w in cast],
        scratch_shapes=[pltpu.VMEM((tm, d), BF16)],
        compiler_params=_params("parallel", "arbitrary"),
        name="ffn_ln",
    )(x, wg, wu, wd_half, g, b, *cast)
    return out[0], tuple(out[1:])


def _in_proj_kernel(x_ref, win_ref, wq_ref, wkv_ref, qg_ref, kvg_ref, rope_ref,
                    lru_ref, gate_ref, q_ref, k_ref, v_ref, *, lru_w, q_rank, kv_rank):
    xb = x_ref[...].astype(BF16)
    nope, rope, vd = QK_NOPE_DIM, QK_ROPE_DIM, V_HEAD_DIM
    tab = rope_ref[...].T

    def rotate(pe_and_swapped):
        prod = pe_and_swapped * tab
        return prod[:, :rope] + prod[:, rope:]

    lru_ref[...] = _dot(xb, win_ref[:, 0:lru_w])
    gate_ref[...] = jax.nn.gelu(_dot(xb, win_ref[:, lru_w:2 * lru_w])).astype(BF16)

    o = 2 * lru_w
    c_q = _dot(xb, win_ref[:, o:o + q_rank])
    o += q_rank
    c_kv = _dot(xb, win_ref[:, o:o + kv_rank])
    o += kv_rank
    kpe = _dot(xb, win_ref[:, o:o + rope])
    k_rot = rotate(jnp.concatenate([kpe, kpe[:, rope // 2:], kpe[:, :rope // 2]], -1))

    qn = _rms_norm(c_q, qg_ref[...]).astype(BF16)
    kvn = _rms_norm(c_kv, kvg_ref[...]).astype(BF16)
    qscale = (nope + rope) ** -0.5 * LOG2E
    hq = nope + 2 * rope
    for h in range(MLA_HEADS):
        qh = _dot(qn, wq_ref[:, h * hq:(h + 1) * hq])
        qrot = jnp.concatenate([qh[:, :nope], rotate(qh[:, nope:])], -1) * qscale
        q_ref[0, h, 0] = qrot.T.astype(BF16)
        kvh = _dot(kvn, wkv_ref[:, h * (nope + vd):(h + 1) * (nope + vd)])
        k_ref[0, h] = jnp.concatenate([kvh[:, :nope], k_rot], -1).astype(BF16)
        vt = kvh[:, nope:].T.astype(BF16)
        tk = v_ref.shape[-1]
        for c in range(v_ref.shape[2]):
            v_ref[0, h, c, 0:vd] = vt[:, c * tk:(c + 1) * tk]
            v_ref[0, h, c, vd:] = jnp.ones((v_ref.shape[3] - vd, tk), BF16)


def in_proj(x, w_in, w_q, w_kv, q_g, kv_g, rope_tab, *, batch, tm, lru_w):
    n, d = x.shape
    s = n // batch
    spb = s // tm
    q_rank, kv_rank = w_q.shape[0], w_kv.shape[0]
    qk = QK_NOPE_DIM + QK_ROPE_DIM
    tk = min(ATTN_TK, tm)
    const = lambda i: (0, 0)
    head_map = lambda i: (i // spb, 0, i % spb, 0)
    kern = functools.partial(_in_proj_kernel, lru_w=lru_w, q_rank=q_rank, kv_rank=kv_rank)
    return pl.pallas_call(
        kern,
        grid=(n // tm,),
        in_specs=[pl.BlockSpec((tm, d), lambda i: (i, 0)),
                  pl.BlockSpec(w_in.shape, const),
                  pl.BlockSpec(w_q.shape, const),
                  pl.BlockSpec(w_kv.shape, const),
                  pl.BlockSpec((1, q_rank), const),
                  pl.BlockSpec((1, kv_rank), const),
                  pl.BlockSpec((2 * QK_ROPE_DIM, tm), lambda i: (0, i))],
        out_specs=[pl.BlockSpec((tm, lru_w), lambda i: (i, 0)),
                   pl.BlockSpec((tm, lru_w), lambda i: (i, 0)),
                   pl.BlockSpec((1, MLA_HEADS, 1, qk, tm), lambda i: (i // spb, 0, i % spb, 0, 0)),
                   pl.BlockSpec((1, MLA_HEADS, tm, qk), head_map),
                   pl.BlockSpec((1, MLA_HEADS, tm // tk, V_HEAD_DIM + ATTN_ONES_ROWS, tk),
                                lambda i: (i // spb, 0, i % spb, 0, 0))],
        out_shape=[jax.ShapeDtypeStruct((n, lru_w), F32),
                   jax.ShapeDtypeStruct((n, lru_w), BF16),
                   jax.ShapeDtypeStruct((batch, MLA_HEADS, spb, qk, tm), BF16),
                   jax.ShapeDtypeStruct((batch, MLA_HEADS, s, qk), BF16),
                   jax.ShapeDtypeStruct((batch, MLA_HEADS, s // tk, V_HEAD_DIM + ATTN_ONES_ROWS, tk), BF16)],
        compiler_params=_params("parallel"),
        name="in_proj",
    )(x, w_in, w_q, w_kv, q_g, kv_g, rope_tab)


def _rglru_kernel(x_ref, gate_ref, cw_ref, cb_ref, wa_ref, ba_ref, wx_ref, bx_ref, lam_ref,
                  y_ref, xbuf, a_scr, u_scr, hout_scr, h_scr, *, ts, pad):
    t = pl.program_id(1)
    w = x_ref.shape[-1]

    @pl.when(t == 0)
    def _():
        xbuf[0:pad, :] = jnp.zeros((pad, w), F32)
        h_scr[0:1, :] = jnp.zeros((1, w), F32)

    xbuf[pad:pad + ts, :] = x_ref[0]
    xc = cb_ref[...] + cw_ref[CONV_WIDTH - 1:CONV_WIDTH, :] * xbuf[pad:pad + ts, :]
    for k in range(CONV_WIDTH - 1):
        sh = CONV_WIDTH - 1 - k
        xc = xc + cw_ref[k:k + 1, :] * xbuf[pad - sh:pad - sh + ts, :]
    xbuf[0:pad, :] = xbuf[ts:ts + pad, :]

    xcb = xc.astype(BF16)
    ngrp = w // MXU_DIM
    rec = jnp.concatenate(
        [_dot(xcb[:, g * MXU_DIM:(g + 1) * MXU_DIM], wa_ref[g]) for g in range(ngrp)], -1)
    ing = jnp.concatenate(
        [_dot(xcb[:, g * MXU_DIM:(g + 1) * MXU_DIM], wx_ref[g]) for g in range(ngrp)], -1)
    rec = _sigmoid(rec + ba_ref[...])
    ing = _sigmoid(ing + bx_ref[...])

    nlam = -lam_ref[...]
    softplus = jnp.maximum(nlam, 0.0) + jnp.log1p(jnp.exp(-jnp.abs(nlam)))
    log_a = -LRU_C * rec * softplus
    a = jnp.exp(log_a)
    mult = jnp.sqrt(jnp.tanh(-log_a) * (1.0 + a * a))
    a_scr[...] = a.reshape(a_scr.shape)
    u_scr[...] = (mult * (ing * xc)).reshape(u_scr.shape)

    def group(g, h):
        for k in range(0, SCAN_ROWS, 2):
            t0, t1 = g * (SCAN_ROWS // SUBLANES) + k // SUBLANES, g * (SCAN_ROWS // SUBLANES) + (k + 1) // SUBLANES
            r0, r1 = pl.ds(k % SUBLANES, 1), pl.ds((k + 1) % SUBLANES, 1)
            a0, u0, a1, u1 = a_scr[t0, r0, :], u_scr[t0, r0, :], a_scr[t1, r1, :], u_scr[t1, r1, :]
            hout_scr[t0, r0, :] = a0 * h + u0
            h = (a1 * a0) * h + (a1 * u0 + u1)
            hout_scr[t1, r1, :] = h
        return h

    h_last = lax.fori_loop(0, ts // SCAN_ROWS, group, h_scr[0:1, :])
    h_scr[0:1, :] = h_last
    y_ref[0] = (gate_ref[0].astype(F32) * hout_scr[...].reshape(ts, w)).astype(BF16)


def rglru(lru_in, gate, conv_w, conv_b, wa_bd, b_a, wx_bd, b_x, lam, *, ts):
    bsz, s, w = lru_in.shape
    pad = 8
    const2 = lambda b, t: (0, 0)
    const3 = lambda b, t: (0, 0, 0)
    kern = functools.partial(_rglru_kernel, ts=ts, pad=pad)
    return pl.pallas_call(
        kern,
        grid=(bsz, s // ts),
        in_specs=[pl.BlockSpec((1, ts, w), lambda b, t: (b, t, 0)),
                  pl.BlockSpec((1, ts, w), lambda b, t: (b, t, 0)),
                  pl.BlockSpec(conv_w.shape, const2),
                  pl.BlockSpec((1, w), const2),
                  pl.BlockSpec(wa_bd.shape, const3),
                  pl.BlockSpec((1, w), const2),
                  pl.BlockSpec(wx_bd.shape, const3),
                  pl.BlockSpec((1, w), const2),
                  pl.BlockSpec((1, w), const2)],
        out_specs=pl.BlockSpec((1, ts, w), lambda b, t: (b, t, 0)),
        out_shape=jax.ShapeDtypeStruct((bsz, s, w), BF16),
        scratch_shapes=[pltpu.VMEM((ts + pad, w), F32),
                        pltpu.VMEM((ts // SUBLANES, SUBLANES, w), F32),
                        pltpu.VMEM((ts // SUBLANES, SUBLANES, w), F32),
                        pltpu.VMEM((ts // SUBLANES, SUBLANES, w), F32),
                        pltpu.VMEM((8, w), F32)],
        compiler_params=_params("parallel", "arbitrary"),
        name="rglru",
    )(lru_in, gate, conv_w, conv_b, wa_bd, b_a, wx_bd, b_x, lam)


def _block_diag_groups(wblk):
    g, d, _ = wblk.shape
    per = MXU_DIM // d
    wg = wblk.reshape(g // per, per, d, d)
    eye = jnp.eye(per, dtype=wblk.dtype)
    out = jnp.einsum('gpij,pq->gpiqj', wg, eye)
    return out.reshape(g // per, MXU_DIM, MXU_DIM)


HEADS_PER_STEP = 2
ATTN_TK = 256
ATTN_ONES_ROWS = 16
ATTN_KV_UNROLL = 2


def _attn_kernel(q_ref, k_ref, vt_ref, o_ref, *scratch, tq):
    nh, nq = q_ref.shape[0], q_ref.shape[1]
    dva, tk = vt_ref.shape[2], vt_ref.shape[3]
    dv = dva - ATTN_ONES_ROWS
    sub = tq // tk
    assert sub % 2 == 0
    heads = range(nh)
    st_scr = [scratch[2 * hh:2 * hh + 2] for hh in heads]
    acc_scr = scratch[2 * nh:]

    def scores(qs, kblk, slot):
        for hh in heads:
            st_scr[hh][slot][...] = _dot(k_ref[hh, pl.ds(kblk * tk, tk), :], qs[hh])

    def update(hh, kblk, slot, diag_off, m):
        st = st_scr[hh][slot][...]
        if diag_off is not None:
            key = lax.broadcasted_iota(jnp.int32, (tk, tq), 0) + diag_off
            qry = lax.broadcasted_iota(jnp.int32, (tk, tq), 1)
            st = jnp.where(key <= qry, st, -jnp.inf)
        m_new = jnp.maximum(m, jnp.max(st, axis=0, keepdims=True))
        pt = jnp.exp2(st - m_new).astype(BF16)
        acc_scr[hh][...] = (jnp.exp2(m - m_new) * acc_scr[hh][...]
                            + _dot(vt_ref[hh, kblk], pt))
        return m_new

    def block(kblk, slot, diag_off, ms, ahead):
        scores(*ahead, 1 - slot)
        return tuple(update(hh, kblk, slot, diag_off, ms[hh]) for hh in heads)

    def load_q(qi):
        return [q_ref[hh, qi] for hh in heads]

    scores(load_q(0), 0, 0)

    def q_tile(qi, _):
        qs = load_q(qi)
        for hh in heads:
            acc_scr[hh][...] = jnp.zeros((dva, tq), F32)

        def kv_tiles(ntiles, kj, ms):
            for c in range(ntiles * sub):
                ms = block(kj * sub + c, c % 2, None, ms, (qs, kj * sub + c + 1))
            return ms

        n_main = qi // ATTN_KV_UNROLL
        ms = tuple(jnp.full((1, tq), -jnp.inf, F32) for _ in heads)
        ms = lax.fori_loop(0, n_main,
                           lambda it, ms: kv_tiles(ATTN_KV_UNROLL, it * ATTN_KV_UNROLL, ms), ms)
        ms = lax.fori_loop(n_main * ATTN_KV_UNROLL, qi, functools.partial(kv_tiles, 1), ms)
        q_next = load_q(jnp.minimum(qi + 1, nq - 1))
        for c in range(sub):
            ahead = (qs, qi * sub + c + 1) if c + 1 < sub else (q_next, 0)
            ms = block(qi * sub + c, c % 2, c * tk, ms, ahead)
        for hh in heads:
            out = acc_scr[hh][0:dv, :] / acc_scr[hh][dv:dv + 1, :]
            o_ref[pl.ds(qi * tq, tq), hh * dv:(hh + 1) * dv] = out.T.astype(o_ref.dtype)
        return 0

    lax.fori_loop(0, nq, q_tile, 0)


def attention(qt, k, vt):
    bsz, nh, nq, dqk, tq = qt.shape
    s = nq * tq
    _, _, nkb, dva, tk = vt.shape
    dv = dva - ATTN_ONES_ROWS
    hps = HEADS_PER_STEP
    return pl.pallas_call(
        functools.partial(_attn_kernel, tq=tq),
        grid=(bsz, nh // hps),
        in_specs=[pl.BlockSpec((None, hps, nq, dqk, tq), lambda b, h: (b, h, 0, 0, 0)),
                  pl.BlockSpec((None, hps, s, dqk), lambda b, h: (b, h, 0, 0)),
                  pl.BlockSpec((None, hps, nkb, dva, tk), lambda b, h: (b, h, 0, 0, 0))],
        out_specs=pl.BlockSpec((None, s, hps * dv), lambda b, h: (b, 0, h)),
        out_shape=jax.ShapeDtypeStruct((bsz, s, nh * dv), BF16),
        scratch_shapes=([pltpu.VMEM((tk, tq), F32)] * (2 * hps) + [pltpu.VMEM((dva, tq), F32)] * hps),
        compiler_params=_params("parallel", "parallel"),
        name="mla_attention",
    )(qt, k, vt)


def _row_chunks(nrows):
    step = min(MM_ROWS, nrows)
    return [slice(r, r + step) for r in range(0, nrows, step)]


def _out_proj_kernel(x_ref, yl_ref, ym_ref, w_ref, g_ref, b_ref, o_ref):
    wl = yl_ref.shape[1]
    for r in _row_chunks(x_ref.shape[0]):
        mix = _dot(yl_ref[r, :], w_ref[0:wl, :]) + _dot(ym_ref[r, :], w_ref[wl:, :])
        o_ref[r, :] = _layer_norm(DEEPNORM_ALPHA * x_ref[r, :] + mix, g_ref[...], b_ref[...])


def out_proj_ln(x, y_lru, y_mla, w_out, g, b, *, tm):
    n, d = x.shape
    wl, wm = y_lru.shape[1], y_mla.shape[1]
    const = lambda i: (0, 0)
    return pl.pallas_call(
        _out_proj_kernel,
        grid=(n // tm,),
        in_specs=[pl.BlockSpec((tm, d), lambda i: (i, 0)),
                  pl.BlockSpec((tm, wl), lambda i: (i, 0)),
                  pl.BlockSpec((tm, wm), lambda i: (i, 0)),
                  pl.BlockSpec((wl + wm, d), const),
                  pl.BlockSpec((1, d), const),
                  pl.BlockSpec((1, d), const)],
        out_specs=pl.BlockSpec((tm, d), lambda i: (i, 0)),
        out_shape=jax.ShapeDtypeStruct((n, d), F32),
        compiler_params=_params("parallel"),
        name="out_proj_ln",
    )(x, y_lru, y_mla, w_out, g, b)


def _ple_kernel(x_ref, p_ref, wg_ref, bg_ref, wp_ref, g_ref, b_ref, o_ref):
    for r in _row_chunks(x_ref.shape[0]):
        x = x_ref[r, :]
        gate = _sigmoid(_dot(x.astype(BF16), wg_ref[...]) + bg_ref[...])
        proj = _dot(p_ref[r, :].astype(BF16), wp_ref[...])
        o_ref[r, :] = _layer_norm(DEEPNORM_ALPHA * x + gate * proj, g_ref[...], b_ref[...])


def ple_ln(x, p, w_gate, b_gate, w_proj, g, b, *, tm):
    n, d = x.shape
    pd = p.shape[1]
    const = lambda i: (0, 0)
    return pl.pallas_call(
        _ple_kernel,
        grid=(n // tm,),
        in_specs=[pl.BlockSpec((tm, d), lambda i: (i, 0)),
                  pl.BlockSpec((tm, pd), lambda i: (i, 0)),
                  pl.BlockSpec((d, d), const),
                  pl.BlockSpec((1, d), const),
                  pl.BlockSpec((pd, d), const),
                  pl.BlockSpec((1, d), const),
                  pl.BlockSpec((1, d), const)],
        out_specs=pl.BlockSpec((tm, d), lambda i: (i, 0)),
        out_shape=jax.ShapeDtypeStruct((n, d), F32),
        compiler_params=_params("parallel"),
        name="ple_ln",
    )(x, p, w_gate, b_gate, w_proj, g, b)


def _swap_halves_cols(w):
    half = w.shape[-1] // 2
    return jnp.concatenate([w[..., half:], w[..., :half]], -1)


def _tile(n, pref):
    t = min(n, pref)
    assert n % t == 0
    return t


def _layer(x, p, rope_tab, prm, *, batch):
    n, d = x.shape
    row = lambda v: v.reshape(1, -1)
    tm = _tile(n // batch, 512)
    tm_ffn = _tile(n, 1024)

    tf = _tile(prm["ffn1_w_gate"].shape[1], 512)
    half = FFN_CAST_SCALES[2]
    x, ffn2_w = ffn_ln(x, prm["ffn1_w_gate"].astype(BF16), prm["ffn1_w_up"].astype(BF16),
                       (half * prm["ffn1_w_down"]).astype(BF16), row(prm["ln1_g"]), row(prm["ln1_b"]),
                       tm=tm_ffn, tf=tf,
                       cast=(prm["ffn2_w_gate"], prm["ffn2_w_up"], prm["ffn2_w_down"]))

    lru_w = prm["conv_w"].shape[1]
    q_rank = prm["w_q_up"].shape[0]
    wq = prm["w_q_up"].astype(BF16).reshape(q_rank, MLA_HEADS, QK_NOPE_DIM + QK_ROPE_DIM)
    wq_ext = jnp.concatenate([wq, _swap_halves_cols(wq[..., QK_NOPE_DIM:])], -1)
    wq_ext = wq_ext.reshape(q_rank, -1)
    lru_in, gate, q, k, vt = in_proj(
        x, prm["w_in"].astype(BF16), wq_ext, prm["w_kv_up"].astype(BF16), row(prm["q_norm_g"]),
        row(prm["kv_norm_g"]), rope_tab, batch=batch, tm=tm, lru_w=lru_w)

    s = n // batch
    y_lru = rglru(lru_in.reshape(batch, s, lru_w), gate.reshape(batch, s, lru_w),
                  prm["conv_w"], row(prm["conv_b"]),
                  _block_diag_groups(prm["lru_w_a"]).astype(BF16), row(prm["lru_b_a"]),
                  _block_diag_groups(prm["lru_w_x"]).astype(BF16), row(prm["lru_b_x"]),
                  row(prm["lru_lambda"]), ts=_tile(s, 1024))
    y_mla = attention(q, k, vt)

    x = out_proj_ln(x, y_lru.reshape(n, lru_w), y_mla.reshape(n, -1), prm["w_out"].astype(BF16),
                    row(prm["ln2_g"]), row(prm["ln2_b"]), tm=tm)

    x, _ = ffn_ln(x, *ffn2_w, row(prm["ln3_g"]), row(prm["ln3_b"]), tm=tm_ffn, tf=tf)

    return ple_ln(x, p, prm["ple_w_gate"].astype(BF16), row(prm["ple_b_gate"]),
                  prm["ple_w_proj"].astype(BF16), row(prm["ln4_g"]), row(prm["ln4_b"]), tm=tm)


def kernel(x, p, positions, ffn1_w_gate, ffn1_w_up, ffn1_w_down, ln1_g, ln1_b, w_in, conv_w, conv_b, lru_w_a, lru_b_a, lru_w_x, lru_b_x, lru_lambda, q_norm_g, w_q_up, kv_norm_g, w_kv_up, w_out, ln2_g, ln2_b, ffn2_w_gate, ffn2_w_up, ffn2_w_down, ln3_g, ln3_b, ple_w_gate, ple_b_gate, ple_w_proj, ln4_g, ln4_b):
    names = ("ffn1_w_gate ffn1_w_up ffn1_w_down ln1_g ln1_b w_in conv_w conv_b lru_w_a lru_b_a lru_w_x "
             "lru_b_x lru_lambda q_norm_g w_q_up kv_norm_g w_kv_up w_out ln2_g ln2_b ffn2_w_gate ffn2_w_up "
             "ffn2_w_down ln3_g ln3_b ple_w_gate ple_b_gate ple_w_proj ln4_g ln4_b").split()
    stacked = dict(zip(names, (ffn1_w_gate, ffn1_w_up, ffn1_w_down, ln1_g, ln1_b, w_in, conv_w, conv_b,
                               lru_w_a, lru_b_a, lru_w_x, lru_b_x, lru_lambda, q_norm_g, w_q_up, kv_norm_g,
                               w_kv_up, w_out, ln2_g, ln2_b, ffn2_w_gate, ffn2_w_up, ffn2_w_down, ln3_g,
                               ln3_b, ple_w_gate, ple_b_gate, ple_w_proj, ln4_g, ln4_b)))
    batch, seq, d = x.shape
    rope_tab = rope_table(positions, tn=_tile(batch * seq, 4096))
    h = x.reshape(batch * seq, d)
    assert ffn1_w_gate.shape[0] == DEPTH
    for i in range(DEPTH):
        prm = {kname: val[i] for kname, val in stacked.items()}
        h = _layer(h, p[i].reshape(batch * seq, -1), rope_tab, prm, batch=batch)
    return h.reshape(batch, seq, d)
```
